```python
import math
import jax, jax.numpy as jnp
from jax import lax
import numpy as np

D_MODEL = 4096
BATCH = 32
SEQ = 256
DEPTH = 2
DEC_BATCH = 2
DEC_SEQ = 2048
PAST_LEN = 512

GRID_W = 64
N_BRANCH = 4
W_BRANCH = D_MODEL // N_BRANCH
D_FF = 2 * D_MODEL
N_MOD = 9
CONV_W = 4
CHUNK = 128
W_RG = W_BRANCH
RG_BLOCKS = 8
RG_BW = W_RG // RG_BLOCKS
RG_C = 8.0
W_HY = W_BRANCH
HY_ORDER = 2
HY_CONV = 3
HY_BANDS = 16
HY_EMB = 2 * HY_BANDS + 1
HY_FH = 64
HY_NF = HY_ORDER * 2 * W_HY
SSD_DI = W_BRANCH
SSD_P = 64
SSD_H = SSD_DI // SSD_P
SSD_N = 128
SSD_G = 2
SSD_CONV_CH = SSD_DI + 2 * SSD_G * SSD_N
W_ML = W_BRANCH
ML_NH = 4
ML_DH = W_ML // ML_NH
ROPE_BASE = 10000.0

ALPHA = (2 * DEPTH) ** 0.25
BETA = (8 * DEPTH) ** -0.25

IN_SIZES = (W_RG, W_RG, 3 * W_HY, SSD_DI, SSD_CONV_CH, 2 * SSD_H, 4 * W_ML, 4 * ML_NH, N_BRANCH * D_MODEL)
IN_OFFSETS = tuple(int(s) for s in np.cumsum(IN_SIZES)[:-1])
N_IN = int(sum(IN_SIZES))

kernel_name = "hybrid_bidir_diffusion_trunk_step"

F32 = jnp.float32


def layer_norm(x, g, b, eps=1e-5):
    xf = x.astype(F32)
    mu = jnp.mean(xf, axis=-1, keepdims=True)
    var = jnp.mean(jnp.square(xf - mu), axis=-1, keepdims=True)
    return ((xf - mu) * lax.rsqrt(var + eps)).astype(x.dtype) * g + b


def rms_norm(x, w, eps=1e-6):
    xf = x.astype(F32)
    return (xf * lax.rsqrt(jnp.mean(xf * xf, axis=-1, keepdims=True) + eps)).astype(x.dtype) * w


def head_norm(h, w, eps=1e-5):
    hf = h.astype(F32)
    mu = jnp.mean(hf, axis=-1, keepdims=True)
    var = jnp.mean(jnp.square(hf - mu), axis=-1, keepdims=True)
    return ((hf - mu) * lax.rsqrt(var + eps)).astype(h.dtype) * w.reshape(ML_NH, ML_DH)


def dw_conv(x, w, b, pad_l, pad_r):
    y = lax.conv_general_dilated(x, w[:, None, :], window_strides=(1,), padding=[(pad_l, pad_r)],
                                 dimension_numbers=('NWC', 'WIO', 'NWC'), feature_group_count=x.shape[-1])
    return y + b


def swiglu(h, wg, wu, wd):
    return (jax.nn.silu(h @ wg) * (h @ wu)) @ wd


def _rev(t):
    return jnp.flip(t, axis=1)


def _chunks(t):
    bn, L = t.shape[:2]
    return jnp.moveaxis(t.reshape((bn, L // CHUNK, CHUNK) + t.shape[2:]), 1, 0)


def _unchunk(t):
    t = jnp.moveaxis(t, 0, 1)
    return t.reshape((t.shape[0], t.shape[1] * t.shape[2]) + t.shape[3:])


def _lin_combine(e1, e2):
    a1, b1 = e1
    a2, b2 = e2
    return a1 * a2, a2 * b1 + b2


def rglru_scan(x, gw, gb, lam, h0):
    bn, L, W = x.shape
    g = jnp.einsum('blnc,gncd->gblnd', x.reshape(bn, L, RG_BLOCKS, RG_BW), gw).reshape(2, bn, L, W)
    g = g + gb[:, None, None, :]
    r, i = jax.nn.sigmoid(g[0]), jax.nn.sigmoid(g[1])
    log_a = -RG_C * r * jax.nn.softplus(-lam)
    a = jnp.exp(log_a)
    u = jnp.sqrt(-jnp.expm1(2.0 * log_a)) * (i * x)
    a_cum, b_cum = lax.associative_scan(_lin_combine, (a, u), axis=1)
    h = a_cum * h0[:, None, :] + b_cum
    return h, h[:, -1]


def hyena_mixer(u, lp):
    L = u.shape[1]
    u = dw_conv(u, lp['hy_conv_w'], lp['hy_conv_b'], 1, 1)
    v, x1, x2 = jnp.split(u, 3, axis=-1)
    t = jnp.arange(L, dtype=F32) / L
    ang = 2.0 * math.pi * t[:, None] * jnp.arange(1, HY_BANDS + 1, dtype=F32)[None, :]
    emb = jnp.concatenate([t[:, None], jnp.cos(ang), jnp.sin(ang)], axis=-1).astype(u.dtype)
    f = jnp.sin(emb @ lp['hy_w1'] + lp['hy_b1'])
    f = jnp.sin(f @ lp['hy_w2'] + lp['hy_b2'])
    f = (f @ lp['hy_w3']) * jnp.exp(-lp['hy_decay'][None, :] * t[:, None].astype(u.dtype))
    f = f.astype(F32).reshape(L, HY_ORDER, 2, W_HY)
    z = v
    for o, gate in enumerate((x1, x2)):
        hf, hb = f[:, o, 0], f[:, o, 1]
        filt = jnp.concatenate([hf, jnp.zeros_like(hf[:1]), jnp.flip(hb[1:], axis=0)], axis=0)
        filt = filt / (jnp.sum(jnp.abs(filt), axis=0, keepdims=True) + 1e-6)
        ff = jnp.fft.rfft(filt, axis=0)
        zf = jnp.fft.rfft(z.astype(F32), n=2 * L, axis=1)
        zc = jnp.fft.irfft(zf * ff[None], n=2 * L, axis=1)[:, :L].astype(u.dtype)
        z = gate * (zc + lp['hy_bias'][o] * z)
    return z


def ssd_scan(x, dt, A, bm, cm, S0):
    mask = jnp.tril(jnp.ones((CHUNK, CHUNK), dtype=bool))[None, :, :, None]

    def step(S, inp):
        xc, dtc, bc, cc = inp
        cs = jnp.cumsum(dtc * A, axis=1)
        seg = cs[:, :, None, :] - cs[:, None, :, :]
        decay = jnp.exp(jnp.where(mask, seg, -jnp.inf))
        scores = jnp.einsum('bthn,bshn->btsh', cc, bc) * decay
        y = jnp.einsum('btsh,bsh,bshp->bthp', scores, dtc, xc)
        y = y + jnp.einsum('bthn,bhpn->bthp', cc, S) * jnp.exp(cs)[..., None]
        w_end = jnp.exp(cs[:, -1:, :] - cs) * dtc
        S_new = jnp.exp(cs[:, -1, :])[:, :, None, None] * S + jnp.einsum('bshn,bsh,bshp->bhpn', bc, w_end, xc)
        return S_new.astype(S.dtype), y.astype(xc.dtype)

    S_last, ys = lax.scan(step, S0, tuple(_chunks(t) for t in (x, dt, bm, cm)))
    return _unchunk(ys), S_last


def mlstm_scan(q, k, v, i_pre, f_pre, C0, n0, m0):
    mask = jnp.tril(jnp.ones((CHUNK, CHUNK), dtype=bool))[None, :, :, None]

    def step(carry, inp):
        C, n, m = carry
        qc, kc, vc, ic, fc = (t.astype(F32) for t in inp)
        Cf, nf, mf = C.astype(F32), n.astype(F32), m.astype(F32)
        b = jnp.cumsum(jax.nn.log_sigmoid(fc), axis=1)
        dmat = jnp.where(mask, b[:, :, None, :] - b[:, None, :, :] + ic[:, None, :, :], -jnp.inf)
        g0 = b + mf[:, None, :]
        mt = jnp.maximum(g0, jnp.max(dmat, axis=2))
        w = jnp.exp(dmat - mt[:, :, None, :])
        w0 = jnp.exp(g0 - mt)
        s = jnp.einsum('bthd,bshd->btsh', qc, kc) * w
        num = jnp.einsum('btsh,bshd->bthd', s, vc) + w0[..., None] * jnp.einsum('bhed,bthd->bthe', Cf, qc)
        den = jnp.sum(s, axis=2) + w0 * jnp.einsum('bhd,bthd->bth', nf, qc)
        h = num / jnp.maximum(jnp.abs(den), jnp.exp(-mt))[..., None]
        b_end = b[:, -1]
        d_end = b_end[:, None, :] - b + ic
        g0_end = b_end + mf
        m_new = jnp.maximum(g0_end, jnp.max(d_end, axis=1))
        we = jnp.exp(d_end - m_new[:, None, :])
        w0e = jnp.exp(g0_end - m_new)
        C_new = w0e[..., None, None] * Cf + jnp.einsum('bsh,bshe,bshd->bhed', we, vc, kc)
        n_new = w0e[..., None] * nf + jnp.einsum('bsh,bshd->bhd', we, kc)
        return (C_new.astype(C.dtype), n_new.astype(n.dtype), m_new.astype(m.dtype)), h.astype(inp[0].dtype)

    (C, n, m), hs = lax.scan(step, (C0, n0, m0), tuple(_chunks(t) for t in (q, k, v, i_pre, f_pre)))
    return _unchunk(hs), (C, n, m)


def rope_2d(t, rows, cols):
    half = t.shape[-1] // 2

    def rot(u, pos):
        nf = u.shape[-1] // 2
        freqs = ROPE_BASE ** (-jnp.arange(nf, dtype=F32) / nf)
        ang = pos.astype(F32)[:, None] * freqs[None, :]
        cos = jnp.cos(ang)[None, :, None, :].astype(u.dtype)
        sin = jnp.sin(ang)[None, :, None, :].astype(u.dtype)
        u1, u2 = u[..., :nf], u[..., nf:]
        return jnp.concatenate([u1 * cos - u2 * sin, u1 * sin + u2 * cos], axis=-1)

    return jnp.concatenate([rot(t[..., :half], rows), rot(t[..., half:], cols)], axis=-1)


def token_mixers(h, st0, lp, grid):
    h0_rg, S0_ssd, C0, n0, m0 = st0
    bn, L, _ = h.shape
    proj = h @ lp['w_in']
    rg_x, rg_y, hy_u, ssd_z, ssd_xbc, ssd_dt, ml_qkvo, ml_if, merge_g = jnp.split(proj, IN_OFFSETS, axis=-1)

    xa = dw_conv(rg_x, lp['rg_conv_w'], lp['rg_conv_b'], 2, 1)
    ha_f, hl_f = rglru_scan(xa, lp['rg_gate_w'][0], lp['rg_gate_b'][0], lp['rg_lambda'][0], h0_rg[:, 0])
    ha_b, hl_b = rglru_scan(_rev(xa), lp['rg_gate_w'][1], lp['rg_gate_b'][1], lp['rg_lambda'][1], h0_rg[:, 1])
    out_a = (ha_f + _rev(ha_b)) * jax.nn.gelu(rg_y)

    out_b = hyena_mixer(hy_u, lp)

    xbc = jax.nn.silu(dw_conv(ssd_xbc, lp['ssd_conv_w'], lp['ssd_conv_b'], 2, 1))
    xs, bm, cm = jnp.split(xbc, [SSD_DI, SSD_DI + SSD_G * SSD_N], axis=-1)
    xs = xs.reshape(bn, L, SSD_H, SSD_P)
    bm = jnp.repeat(bm.reshape(bn, L, SSD_G, SSD_N), SSD_H // SSD_G, axis=2)
    cm = jnp.repeat(cm.reshape(bn, L, SSD_G, SSD_N), SSD_H // SSD_G, axis=2)
    dt = jax.nn.softplus(ssd_dt.reshape(bn, L, 2, SSD_H) + lp['ssd_dt_bias'])
    A = -jnp.exp(lp['ssd_A_log'])
    yc_f, S_f = ssd_scan(xs, dt[:, :, 0], A[0], bm, cm, S0_ssd[:, 0])
    yc_b, S_b = ssd_scan(_rev(xs), _rev(dt[:, :, 1]), A[1], _rev(bm), _rev(cm), S0_ssd[:, 1])
    yc = yc_f + _rev(yc_b) + lp['ssd_D'][:, None] * xs
    out_c = rms_norm(yc.reshape(bn, L, SSD_DI) * jax.nn.silu(ssd_z), lp['ssd_norm_w'])

    q, k, v, o = jnp.split(ml_qkvo, 4, axis=-1)
    hd = (bn, L, ML_NH, ML_DH)
    q, k, v = q.reshape(hd), k.reshape(hd) * (ML_DH ** -0.5), v.reshape(hd)
    if grid is not None:
        q, k = rope_2d(q, *grid), rope_2d(k, *grid)
    gt = ml_if.reshape(bn, L, 2, 2, ML_NH) + lp['ml_gate_b']
    hd_f, (C_f, n_f, m_f) = mlstm_scan(q, k, v, gt[:, :, 0, 0], gt[:, :, 0, 1], C0[:, 0], n0[:, 0], m0[:, 0])
    hd_b, (C_b, n_b, m_b) = mlstm_scan(_rev(q), _rev(k), _rev(v), _rev(gt[:, :, 1, 0]), _rev(gt[:, :, 1, 1]),
                                       C0[:, 1], n0[:, 1], m0[:, 1])
    out_d = jax.nn.sigmoid(o) * head_norm(hd_f + _rev(hd_b), lp['ml_norm_w']).reshape(bn, L, W_ML)

    gates = jnp.split(merge_g, N_BRANCH, axis=-1)
    merged = 0.0
    for j, br in enumerate((out_a, out_b, out_c, out_d)):
        merged = merged + jax.nn.sigmoid(gates[j]) * (br @ lp['branch_w'][j])
    out = merged @ lp['mix_out']
    states = (jnp.stack([hl_f, hl_b], axis=1), jnp.stack([S_f, S_b], axis=1), jnp.stack([C_f, C_b], axis=1),
              jnp.stack([n_f, n_b], axis=1), jnp.stack([m_f, m_b], axis=1))
    return out, states


def trunk_layer(x, mod, st0, lp, grid):
    m = [mod[:, j, None, :] for j in range(N_MOD)]
    x = layer_norm(ALPHA * x + 0.5 * m[2] * swiglu(x * (1 + m[1]) + m[0], lp['ffn_wg'][0], lp['ffn_wu'][0], lp['ffn_wd'][0]),
                   lp['ln_g'][0], lp['ln_b'][0])
    mix, st = token_mixers(x * (1 + m[4]) + m[3], st0, lp, grid)
    x = layer_norm(ALPHA * x + m[5] * mix, lp['ln_g'][1], lp['ln_b'][1])
    x = layer_norm(ALPHA * x + 0.5 * m[8] * swiglu(x * (1 + m[7]) + m[6], lp['ffn_wg'][1], lp['ffn_wu'][1], lp['ffn_wd'][1]),
                   lp['ln_g'][2], lp['ln_b'][2])
    return x, st


def setup_inputs(seed: int = 0) -> dict:
    key = jax.random.key(seed)
    ks = iter(jax.random.split(key, 64))

    def nrm(shape, s):
        return jax.random.normal(next(ks), shape, jnp.float32) * s

    def unif(shape, lo, hi):
        return jax.random.uniform(next(ks), shape, jnp.float32, lo, hi)

    a_rg = unif((DEPTH, 2, W_RG), 0.9, 0.999)
    dt0 = jnp.exp(unif((DEPTH, 2, SSD_H), math.log(1e-3), math.log(1e-1)))
    return {
        "x_prompt": nrm((BATCH, SEQ, D_MODEL), 1.0),
        "x_sample": nrm((DEC_BATCH, DEC_SEQ, D_MODEL), 1.0),
        "state_rglru": nrm((DEC_BATCH, DEPTH, 2, W_RG), 0.5),
        "state_ssd": nrm((DEC_BATCH, DEPTH, 2, SSD_H, SSD_P, SSD_N), 0.3),
        "state_mlstm_C": nrm((DEC_BATCH, DEPTH, 2, ML_NH, ML_DH, ML_DH), 0.3),
        "state_mlstm_n": nrm((DEC_BATCH, DEPTH, 2, ML_NH, ML_DH), 0.3),
        "state_mlstm_m": unif((DEC_BATCH, DEPTH, 2, ML_NH), 0.0, 2.0),
        "c": nrm((DEC_BATCH, D_MODEL), 1.0),
        "c_ctx": nrm((D_MODEL,), 1.0),
        "ada_w": nrm((DEPTH, D_MODEL, N_MOD * D_MODEL), 0.5 * D_MODEL ** -0.5),
        "ada_b": nrm((DEPTH, N_MOD * D_MODEL), 0.02),
        "ln_g": 1.0 + nrm((DEPTH, 3, D_MODEL), 0.05),
        "ln_b": nrm((DEPTH, 3, D_MODEL), 0.02),
        "ffn_wg": nrm((DEPTH, 2, D_MODEL, D_FF), D_MODEL ** -0.5),
        "ffn_wu": nrm((DEPTH, 2, D_MODEL, D_FF), D_MODEL ** -0.5),
        "ffn_wd": nrm((DEPTH, 2, D_FF, D_MODEL), BETA * D_FF ** -0.5),
        "w_in": nrm((DEPTH, D_MODEL, N_IN), D_MODEL ** -0.5),
        "rg_conv_w": nrm((DEPTH, CONV_W, W_RG), CONV_W ** -0.5),
        "rg_conv_b": nrm((DEPTH, W_RG), 0.02),
        "rg_gate_w": nrm((DEPTH, 2, 2, RG_BLOCKS, RG_BW, RG_BW), RG_BW ** -0.5),
        "rg_gate_b": nrm((DEPTH, 2, 2, W_RG), 0.02),
        "rg_lambda": jnp.log(a_rg) - jnp.log1p(-a_rg),
        "hy_conv_w": nrm((DEPTH, HY_CONV, 3 * W_HY), HY_CONV ** -0.5),
        "hy_conv_b": nrm((DEPTH, 3 * W_HY), 0.02),
        "hy_w1": nrm((DEPTH, HY_EMB, HY_FH), HY_EMB ** -0.5),
        "hy_b1": nrm((DEPTH, HY_FH), 0.02),
        "hy_w2": nrm((DEPTH, HY_FH, HY_FH), HY_FH ** -0.5),
        "hy_b2": nrm((DEPTH, HY_FH), 0.02),
        "hy_w3": nrm((DEPTH, HY_FH, HY_NF), HY_FH ** -0.5),
        "hy_decay": unif((DEPTH, HY_NF), 1.0, 20.0),
        "hy_bias": nrm((DEPTH, HY_ORDER, W_HY), 0.1),
        "ssd_conv_w": nrm((DEPTH, CONV_W, SSD_CONV_CH), CONV_W ** -0.5),
        "ssd_conv_b": nrm((DEPTH, SSD_CONV_CH), 0.02),
        "ssd_dt_bias": dt0 + jnp.log(-jnp.expm1(-dt0)),
        "ssd_A_log": jnp.log(unif((DEPTH, 2, SSD_H), 1.0, 16.0)),
        "ssd_D": 1.0 + nrm((DEPTH, SSD_H), 0.1),
        "ssd_norm_w": 1.0 + nrm((DEPTH, SSD_DI), 0.05),
        "ml_gate_b": jnp.concatenate([nrm((DEPTH, 2, 1, ML_NH), 0.1), unif((DEPTH, 2, 1, ML_NH), 3.0, 6.0)], axis=2),
        "ml_norm_w": 1.0 + nrm((DEPTH, W_ML), 0.05),
        "branch_w": nrm((DEPTH, N_BRANCH, W_BRANCH, D_MODEL), W_BRANCH ** -0.5),
        "mix_out": nrm((DEPTH, D_MODEL, D_MODEL), BETA * D_MODEL ** -0.5),
    }


def reference(x_prompt, x_sample, state_rglru, state_ssd, state_mlstm_C, state_mlstm_n, state_mlstm_m, c, c_ctx,
              ada_w, ada_b, ln_g, ln_b, ffn_wg, ffn_wu, ffn_wd, w_in, rg_conv_w, rg_conv_b, rg_gate_w, rg_gate_b,
              rg_lambda, hy_conv_w, hy_conv_b, hy_w1, hy_b1, hy_w2, hy_b2, hy_w3, hy_decay, hy_bias,
              ssd_conv_w, ssd_conv_b, ssd_dt_bias, ssd_A_log, ssd_D, ssd_norm_w, ml_gate_b, ml_norm_w,
              branch_w, mix_out):
    bp = x_prompt.shape[0]
    bs, ls = x_sample.shape[:2]
    n_rows = ls // GRID_W
    rows = jnp.repeat(jnp.arange(n_rows), GRID_W)
    cols = jnp.arange(n_rows * GRID_W) % GRID_W
    dtp = x_prompt.dtype
    zero_st = (jnp.zeros((bp, 2, W_RG), dtp), jnp.zeros((bp, 2, SSD_H, SSD_P, SSD_N), dtp),
               jnp.zeros((bp, 2, ML_NH, ML_DH, ML_DH), dtp), jnp.zeros((bp, 2, ML_NH, ML_DH), dtp),
               jnp.zeros((bp, 2, ML_NH), dtp))
    yp, ys = x_prompt, x_sample
    new_rg, new_ssd, new_C, new_n, new_m = [], [], [], [], []
    for l in range(DEPTH):
        lp = dict(ln_g=ln_g[l], ln_b=ln_b[l], ffn_wg=ffn_wg[l], ffn_wu=ffn_wu[l], ffn_wd=ffn_wd[l], w_in=w_in[l],
                  rg_conv_w=rg_conv_w[l], rg_conv_b=rg_conv_b[l], rg_gate_w=rg_gate_w[l], rg_gate_b=rg_gate_b[l],
                  rg_lambda=rg_lambda[l], hy_conv_w=hy_conv_w[l], hy_conv_b=hy_conv_b[l], hy_w1=hy_w1[l],
                  hy_b1=hy_b1[l], hy_w2=hy_w2[l], hy_b2=hy_b2[l], hy_w3=hy_w3[l], hy_decay=hy_decay[l],
                  hy_bias=hy_bias[l], ssd_conv_w=ssd_conv_w[l], ssd_conv_b=ssd_conv_b[l],
                  ssd_dt_bias=ssd_dt_bias[l], ssd_A_log=ssd_A_log[l], ssd_D=ssd_D[l], ssd_norm_w=ssd_norm_w[l],
                  ml_gate_b=ml_gate_b[l], ml_norm_w=ml_norm_w[l], branch_w=branch_w[l], mix_out=mix_out[l])
        mod_ctx = (jax.nn.silu(c_ctx) @ ada_w[l] + ada_b[l]).reshape(1, N_MOD, D_MODEL)
        mod_lat = (jax.nn.silu(c) @ ada_w[l] + ada_b[l]).reshape(bs, N_MOD, D_MODEL)
        yp, st = trunk_layer(yp, mod_ctx, zero_st, lp, None)
        new_rg.append(st[0]); new_ssd.append(st[1]); new_C.append(st[2]); new_n.append(st[3]); new_m.append(st[4])
        cache_l = (state_rglru[:, l], state_ssd[:, l], state_mlstm_C[:, l], state_mlstm_n[:, l], state_mlstm_m[:, l])
        ys, _ = trunk_layer(ys, mod_lat, cache_l, lp, (rows, cols))
    out_rg = jnp.stack(new_rg, axis=1)
    out_ssd = jnp.stack(new_ssd, axis=1)
    out_C = jnp.stack(new_C, axis=1)
    out_n = jnp.stack(new_n, axis=1)
    out_m = jnp.stack(new_m, axis=1)
    return (yp, ys, out_rg, out_ssd, out_C, out_n, out_m)
```

```python
import functools
import math

import jax
import jax.numpy as jnp
import numpy as np
from jax import lax
from jax.experimental import pallas as pl
from jax.experimental.pallas import tpu as pltpu

D_MODEL = 4096
DEPTH = 2
GRID_W = 64
N_BRANCH = 4
W_BRANCH = D_MODEL // N_BRANCH
D_FF = 2 * D_MODEL
N_MOD = 9
CONV_W = 4
CHUNK = 128
W_RG = W_BRANCH
RG_BLOCKS = 8
RG_BW = W_RG // RG_BLOCKS
RG_C = 8.0
W_HY = W_BRANCH
HY_ORDER = 2
HY_BANDS = 16
SSD_DI = W_BRANCH
SSD_P = 64
SSD_H = SSD_DI // SSD_P
SSD_N = 128
SSD_G = 2
SSD_CONV_CH = SSD_DI + 2 * SSD_G * SSD_N
W_ML = W_BRANCH
ML_NH = 4
ML_DH = W_ML // ML_NH
ROPE_BASE = 10000.0
ALPHA = (2 * DEPTH) ** 0.25

IN_SIZES = (W_RG, W_RG, 3 * W_HY, SSD_DI, SSD_CONV_CH, 2 * SSD_H, 4 * W_ML, 4 * ML_NH, N_BRANCH * D_MODEL)
IN_OFFSETS = tuple(int(s) for s in np.cumsum(IN_SIZES)[:-1])

F32 = jnp.float32
BF16 = jnp.bfloat16

O_RGX = 0
O_RGY = O_RGX + W_RG
O_HY = O_RGY + W_RG
O_SSDZ = O_HY + 3 * W_HY
O_XBC = O_SSDZ + SSD_DI
O_QKVO = O_XBC + SSD_CONV_CH
O_SMALL = O_QKVO + 4 * W_ML
N_PROJ = 12288

VMEM_LIMIT = 56 * 1024 * 1024


def _cparams(sem):
    return pltpu.CompilerParams(dimension_semantics=sem, vmem_limit_bytes=VMEM_LIMIT)


def _ada_body(s_ref, w_ref, b_ref, o_ref):
    w = w_ref[...].astype(BF16)
    o_ref[...] = jnp.dot(s_ref[...], w, preferred_element_type=F32) + b_ref[...]


def _ada_call(s, ada_w, ada_b):
    bn = 512
    n = ada_w.shape[-1]
    return pl.pallas_call(
        _ada_body,
        grid=(DEPTH, n // bn),
        in_specs=[pl.BlockSpec((8, D_MODEL), lambda l, j: (0, 0)),
                  pl.BlockSpec((None, D_MODEL, bn), lambda l, j: (l, 0, j)),
                  pl.BlockSpec((None, 1, bn), lambda l, j: (l, 0, j))],
        out_specs=pl.BlockSpec((None, 8, bn), lambda l, j: (l, 0, j)),
        out_shape=jax.ShapeDtypeStruct((DEPTH, 8, n), F32),
        compiler_params=_cparams(("parallel", "parallel")),
        name="ada_mod",
    )(s, ada_w, ada_b)


def _modulate_body(x_ref, sh_ref, sc_ref, o_ref):
    o_ref[...] = (x_ref[...] * (1.0 + sc_ref[...]) + sh_ref[...]).astype(BF16)


def _modulate_call(x, shift, scale, midx):
    m = x.shape[0]
    bm = 256
    mspec = pl.BlockSpec((None, 1, D_MODEL), lambda i: (midx(i, bm), 0, 0))
    return pl.pallas_call(
        _modulate_body,
        grid=(m // bm,),
        in_specs=[pl.BlockSpec((bm, D_MODEL), lambda i: (i, 0)), mspec, mspec],
        out_specs=pl.BlockSpec((bm, D_MODEL), lambda i: (i, 0)),
        out_shape=jax.ShapeDtypeStruct((m, D_MODEL), BF16),
        compiler_params=_cparams(("parallel",)),
        name="modulate",
    )(x, shift, scale)


def _ffn_up_body(h_ref, wg_ref, wu_ref, o_ref):
    h = h_ref[...]
    g = jnp.dot(h, wg_ref[...], preferred_element_type=F32)
    u = jnp.dot(h, wu_ref[...], preferred_element_type=F32)
    o_ref[...] = (g * jax.nn.sigmoid(g) * u).astype(BF16)


def _ffn_up_call(h, wg, wu):
    m = h.shape[0]
    bm, bn = 1024, 512
    return pl.pallas_call(
        _ffn_up_body,
        grid=(m // bm, D_FF // bn),
        in_specs=[pl.BlockSpec((bm, D_MODEL), lambda i, j: (i, 0)),
                  pl.BlockSpec((D_MODEL, bn), lambda i, j: (0, j)),
                  pl.BlockSpec((D_MODEL, bn), lambda i, j: (0, j))],
        out_specs=pl.BlockSpec((bm, bn), lambda i, j: (i, j)),
        out_shape=jax.ShapeDtypeStruct((m, D_FF), BF16),
        compiler_params=_cparams(("parallel", "parallel")),
        name="ffn_up",
    )(h, wg, wu)


def _in_proj_body(h_ref, w_ref, o_ref):
    o_ref[...] = jnp.dot(h_ref[...], w_ref[...], preferred_element_type=F32)


def _in_proj_call(h, w):
    m = h.shape[0]
    bm, bn = 1024, 512
    return pl.pallas_call(
        _in_proj_body,
        grid=(m // bm, N_PROJ // bn),
        in_specs=[pl.BlockSpec((bm, D_MODEL), lambda i, j: (i, 0)),
                  pl.BlockSpec((D_MODEL, bn), lambda i, j: (0, j))],
        out_specs=pl.BlockSpec((bm, bn), lambda i, j: (i, j)),
        out_shape=jax.ShapeDtypeStruct((m, N_PROJ), F32),
        compiler_params=_cparams(("parallel", "parallel")),
        name="in_proj",
    )(h, w)


def _merge_body(h_ref, br_ref, wg0, wg1, wg2, wg3, wb0, wb1, wb2, wb3, o_ref):
    h = h_ref[...]
    acc = None
    for j, (wg, wb) in enumerate(((wg0, wb0), (wg1, wb1), (wg2, wb2), (wg3, wb3))):
        g = jnp.dot(h, wg[...], preferred_element_type=F32)
        b = jnp.dot(br_ref[:, j * W_BRANCH:(j + 1) * W_BRANCH], wb[...], preferred_element_type=F32)
        t = jax.nn.sigmoid(g) * b
        acc = t if acc is None else acc + t
    o_ref[...] = acc.astype(BF16)


def _merge_call(h, br, w_merge, branch_w):
    m = h.shape[0]
    bm, bn = 512, 256
    nb = D_MODEL // bn
    wg_specs = [pl.BlockSpec((D_MODEL, bn), functools.partial(lambda i, n, j: (0, j * nb + n), j=j))
                for j in range(N_BRANCH)]
    wb_specs = [pl.BlockSpec((None, W_BRANCH, bn), functools.partial(lambda i, n, j: (j, 0, n), j=j))
                for j in range(N_BRANCH)]
    return pl.pallas_call(
        _merge_body,
        grid=(m // bm, nb),
        in_specs=[pl.BlockSpec((bm, D_MODEL), lambda i, n: (i, 0)),
                  pl.BlockSpec((bm, D_MODEL), lambda i, n: (i, 0))] + wg_specs + wb_specs,
        out_specs=pl.BlockSpec((bm, bn), lambda i, n: (i, n)),
        out_shape=jax.ShapeDtypeStruct((m, D_MODEL), BF16),
        compiler_params=_cparams(("parallel", "parallel")),
        name="merge",
    )(h, br, *([w_merge] * N_BRANCH), *([branch_w] * N_BRANCH))


def _res_ln_body(a_ref, w_ref, x_ref, gate_ref, lng_ref, lnb_ref, nsh_ref, nsc_ref, y_ref, hn_ref, *,
                 coef, nk, bm, emit_next):
    k = pl.program_id(1)

    @pl.when(k == 0)
    def _():
        y_ref[...] = jnp.zeros_like(y_ref)

    y_ref[...] += jnp.dot(a_ref[...], w_ref[...], preferred_element_type=F32)

    @pl.when(k == nk - 1)
    def _():
        rows = 8
        gate = coef * gate_ref[...]
        lng, lnb = lng_ref[...], lnb_ref[...]
        nsc, nsh = 1.0 + nsc_ref[...], nsh_ref[...]

        def chunk(c, carry):
            r0 = pl.multiple_of(c * rows, rows)
            r = ALPHA * x_ref[pl.ds(r0, rows), :] + gate * y_ref[pl.ds(r0, rows), :]
            mu = jnp.mean(r, axis=-1, keepdims=True)
            d = r - mu
            var = jnp.mean(d * d, axis=-1, keepdims=True)
            o = d * lax.rsqrt(var + 1e-5) * lng + lnb
            y_ref[pl.ds(r0, rows), :] = o
            if emit_next:
                hn_ref[pl.ds(r0, rows), :] = (o * nsc + nsh).astype(BF16)
            return carry

        lax.fori_loop(0, bm // rows, chunk, 0)
        if not emit_next:
            hn_ref[...] = jnp.zeros_like(hn_ref)


def _res_ln_call(a, w, x, gate, lng, lnb, nshift, nscale, midx, coef, emit_next):
    m, kdim = a.shape
    bm, bk = 512, 512
    nk = kdim // bk
    hn_rows = bm if emit_next else 8
    mspec = pl.BlockSpec((None, 1, D_MODEL), lambda i, k: (midx(i, bm), 0, 0))
    vspec = pl.BlockSpec((1, D_MODEL), lambda i, k: (0, 0))
    y, hn = pl.pallas_call(
        functools.partial(_res_ln_body, coef=coef, nk=nk, bm=bm, emit_next=emit_next),
        grid=(m // bm, nk),
        in_specs=[pl.BlockSpec((bm, bk), lambda i, k: (i, k)),
                  pl.BlockSpec((bk, D_MODEL), lambda i, k: (k, 0)),
                  pl.BlockSpec((bm, D_MODEL), lambda i, k: (i, 0)),
                  mspec, vspec, vspec, mspec, mspec],
        out_specs=[pl.BlockSpec((bm, D_MODEL), lambda i, k: (i, 0)),
                   pl.BlockSpec((hn_rows, D_MODEL), lambda i, k: (i, 0))],
        out_shape=[jax.ShapeDtypeStruct((m, D_MODEL), F32),
                   jax.ShapeDtypeStruct((m if emit_next else 8 * (m // bm), D_MODEL), BF16)],
        compiler_params=_cparams(("parallel", "arbitrary")),
        name="res_ln",
    )(a, w, x, gate, lng, lnb, nshift, nscale)
    return y, hn


def _dw_conv(x, w, b, pad_l, pad_r):
    y = lax.conv_general_dilated(x, w[:, None, :], window_strides=(1,), padding=[(pad_l, pad_r)],
                                 dimension_numbers=('NWC', 'WIO', 'NWC'), feature_group_count=x.shape[-1])
    return y + b


def _rev(t):
    return jnp.flip(t, axis=1)


def _chunks(t):
    bn, L = t.shape[:2]
    return jnp.moveaxis(t.reshape((bn, L // CHUNK, CHUNK) + t.shape[2:]), 1, 0)


def _unchunk(t):
    t = jnp.moveaxis(t, 0, 1)
    return t.reshape((t.shape[0], t.shape[1] * t.shape[2]) + t.shape[3:])


def _lin_combine(e1, e2):
    a1, b1 = e1
    a2, b2 = e2
    return a1 * a2, a2 * b1 + b2


def _rglru_scan(x, gw, gb, lam, h0):
    bn, L, W = x.shape
    g = jnp.einsum('blnc,gncd->gblnd', x.reshape(bn, L, RG_BLOCKS, RG_BW), gw).reshape(2, bn, L, W)
    g = g + gb[:, None, None, :]
    r, i = jax.nn.sigmoid(g[0]), jax.nn.sigmoid(g[1])
    log_a = -RG_C * r * jax.nn.softplus(-lam)
    a = jnp.exp(log_a)
    u = jnp.sqrt(-jnp.expm1(2.0 * log_a)) * (i * x)
    a_cum, b_cum = lax.associative_scan(_lin_combine, (a, u), axis=1)
    h = a_cum * h0[:, None, :] + b_cum
    return h, h[:, -1]


def _hyena_mixer(u, lp):
    L = u.shape[1]
    u = _dw_conv(u, lp['hy_conv_w'], lp['hy_conv_b'], 1, 1)
    v, x1, x2 = jnp.split(u, 3, axis=-1)
    t = jnp.arange(L, dtype=F32) / L
    ang = 2.0 * math.pi * t[:, None] * jnp.arange(1, HY_BANDS + 1, dtype=F32)[None, :]
    emb = jnp.concatenate([t[:, None], jnp.cos(ang), jnp.sin(ang)], axis=-1).astype(u.dtype)
    f = jnp.sin(emb @ lp['hy_w1'] + lp['hy_b1'])
    f = jnp.sin(f @ lp['hy_w2'] + lp['hy_b2'])
    f = (f @ lp['hy_w3']) * jnp.exp(-lp['hy_decay'][None, :] * t[:, None].astype(u.dtype))
    f = f.astype(F32).reshape(L, HY_ORDER, 2, W_HY)
    z = v
    for o, gate in enumerate((x1, x2)):
        hf, hb = f[:, o, 0], f[:, o, 1]
        filt = jnp.concatenate([hf, jnp.zeros_like(hf[:1]), jnp.flip(hb[1:], axis=0)], axis=0)
        filt = filt / (jnp.sum(jnp.abs(filt), axis=0, keepdims=True) + 1e-6)
        ff = jnp.fft.rfft(filt, axis=0)
        zf = jnp.fft.rfft(z.astype(F32), n=2 * L, axis=1)
        zc = jnp.fft.irfft(zf * ff[None], n=2 * L, axis=1)[:, :L].astype(u.dtype)
        z = gate * (zc + lp['hy_bias'][o] * z)
    return z


def _ssd_scan(x, dt, A, bm, cm, S0):
    mask = jnp.tril(jnp.ones((CHUNK, CHUNK), dtype=bool))[None, :, :, None]

    def step(S, inp):
        xc, dtc, bc, cc = inp
        cs = jnp.cumsum(dtc * A, axis=1)
        seg = cs[:, :, None, :] - cs[:, None, :, :]
        decay = jnp.exp(jnp.where(mask, seg, -jnp.inf))
        scores = jnp.einsum('bthn,bshn->btsh', cc, bc) * decay
        y = jnp.einsum('btsh,bsh,bshp->bthp', scores, dtc, xc)
        y = y + jnp.einsum('bthn,bhpn->bthp', cc, S) * jnp.exp(cs)[..., None]
        w_end = jnp.exp(cs[:, -1:, :] - cs) * dtc
        S_new = jnp.exp(cs[:, -1, :])[:, :, None, None] * S + jnp.einsum('bshn,bsh,bshp->bhpn', bc, w_end, xc)
        return S_new.astype(S.dtype), y.astype(xc.dtype)

    S_last, ys = lax.scan(step, S0, tuple(_chunks(t) for t in (x, dt, bm, cm)))
    return _unchunk(ys), S_last


def _mlstm_scan(q, k, v, i_pre, f_pre, C0, n0, m0):
    mask = jnp.tril(jnp.ones((CHUNK, CHUNK), dtype=bool))[None, :, :, None]

    def step(carry, inp):
        C, n, m = carry
        qc, kc, vc, ic, fc = inp
        b = jnp.cumsum(jax.nn.log_sigmoid(fc), axis=1)
        dmat = jnp.where(mask, b[:, :, None, :] - b[:, None, :, :] + ic[:, None, :, :], -jnp.inf)
        g0 = b + m[:, None, :]
        mt = jnp.maximum(g0, jnp.max(dmat, axis=2))
        w = jnp.exp(dmat - mt[:, :, None, :])
        w0 = jnp.exp(g0 - mt)
        s = jnp.einsum('bthd,bshd->btsh', qc, kc) * w
        num = jnp.einsum('btsh,bshd->bthd', s, vc) + w0[..., None] * jnp.einsum('bhed,bthd->bthe', C, qc)
        den = jnp.sum(s, axis=2) + w0 * jnp.einsum('bhd,bthd->bth', n, qc)
        h = num / jnp.maximum(jnp.abs(den), jnp.exp(-mt))[..., None]
        b_end = b[:, -1]
        d_end = b_end[:, None, :] - b + ic
        g0_end = b_end + m
        m_new = jnp.maximum(g0_end, jnp.max(d_end, axis=1))
        we = jnp.exp(d_end - m_new[:, None, :])
        w0e = jnp.exp(g0_end - m_new)
        C_new = w0e[..., None, None] * C + jnp.einsum('bsh,bshe,bshd->bhed', we, vc, kc)
        n_new = w0e[..., None] * n + jnp.einsum('bsh,bshd->bhd', we, kc)
        return (C_new, n_new, m_new), h

    (C, n, m), hs = lax.scan(step, (C0, n0, m0), tuple(_chunks(t) for t in (q, k, v, i_pre, f_pre)))
    return _unchunk(hs), (C, n, m)


def _rope_2d(t, rows, cols):
    half = t.shape[-1] // 2

    def rot(u, pos):
        nf = u.shape[-1] // 2
        freqs = ROPE_BASE ** (-jnp.arange(nf, dtype=F32) / nf)
        ang = pos.astype(F32)[:, None] * freqs[None, :]
        cos = jnp.cos(ang)[None, :, None, :]
        sin = jnp.sin(ang)[None, :, None, :]
        u1, u2 = u[..., :nf], u[..., nf:]
        return jnp.concatenate([u1 * cos - u2 * sin, u1 * sin + u2 * cos], axis=-1)

    return jnp.concatenate([rot(t[..., :half], rows), rot(t[..., half:], cols)], axis=-1)


def _rms_norm(x, w, eps=1e-6):
    return x * lax.rsqrt(jnp.mean(x * x, axis=-1, keepdims=True) + eps) * w


def _head_norm(h, w, eps=1e-5):
    mu = jnp.mean(h, axis=-1, keepdims=True)
    var = jnp.mean(jnp.square(h - mu), axis=-1, keepdims=True)
    return (h - mu) * lax.rsqrt(var + eps) * w.reshape(ML_NH, ML_DH)


def _mixers_jax(proj, st0, lp, grid):
    h0_rg, S0_ssd, C0, n0, m0 = st0
    bn, L, _ = proj.shape
    rg_x = proj[..., O_RGX:O_RGX + W_RG]
    rg_y = proj[..., O_RGY:O_RGY + W_RG]
    hy_u = proj[..., O_HY:O_HY + 3 * W_HY]
    ssd_z = proj[..., O_SSDZ:O_SSDZ + SSD_DI]
    ssd_xbc = proj[..., O_XBC:O_XBC + SSD_CONV_CH]
    ml_qkvo = proj[..., O_QKVO:O_QKVO + 4 * W_ML]
    ssd_dt = proj[..., O_SMALL:O_SMALL + 2 * SSD_H]
    ml_if = proj[..., O_SMALL + 2 * SSD_H:O_SMALL + 2 * SSD_H + 4 * ML_NH]

    xa = _dw_conv(rg_x, lp['rg_conv_w'], lp['rg_conv_b'], 2, 1)
    ha_f, hl_f = _rglru_scan(xa, lp['rg_gate_w'][0], lp['rg_gate_b'][0], lp['rg_lambda'][0], h0_rg[:, 0])
    ha_b, hl_b = _rglru_scan(_rev(xa), lp['rg_gate_w'][1], lp['rg_gate_b'][1], lp['rg_lambda'][1], h0_rg[:, 1])
    out_a = (ha_f + _rev(ha_b)) * jax.nn.gelu(rg_y)

    out_b = _hyena_mixer(hy_u, lp)

    xbc = jax.nn.silu(_dw_conv(ssd_xbc, lp['ssd_conv_w'], lp['ssd_conv_b'], 2, 1))
    xs, bm, cm = jnp.split(xbc, [SSD_DI, SSD_DI + SSD_G * SSD_N], axis=-1)
    xs = xs.reshape(bn, L, SSD_H, SSD_P)
    bm = jnp.repeat(bm.reshape(bn, L, SSD_G, SSD_N), SSD_H // SSD_G, axis=2)
    cm = jnp.repeat(cm.reshape(bn, L, SSD_G, SSD_N), SSD_H // SSD_G, axis=2)
    dt = jax.nn.softplus(ssd_dt.reshape(bn, L, 2, SSD_H) + lp['ssd_dt_bias'])
    A = -jnp.exp(lp['ssd_A_log'])
    yc_f, S_f = _ssd_scan(xs, dt[:, :, 0], A[0], bm, cm, S0_ssd[:, 0])
    yc_b, S_b = _ssd_scan(_rev(xs), _rev(dt[:, :, 1]), A[1], _rev(bm), _rev(cm), S0_ssd[:, 1])
    yc = yc_f + _rev(yc_b) + lp['ssd_D'][:, None] * xs
    out_c = _rms_norm(yc.reshape(bn, L, SSD_DI) * jax.nn.silu(ssd_z), lp['ssd_norm_w'])

    q, k, v, o = jnp.split(ml_qkvo, 4, axis=-1)
    hd = (bn, L, ML_NH, ML_DH)
    q, k, v = q.reshape(hd), k.reshape(hd) * (ML_DH ** -0.5), v.reshape(hd)
    if grid is not None:
        q, k = _rope_2d(q, *grid), _rope_2d(k, *grid)
    gt = ml_if.reshape(bn, L, 2, 2, ML_NH) + lp['ml_gate_b']
    hd_f, (C_f, n_f, m_f) = _mlstm_scan(q, k, v, gt[:, :, 0, 0], gt[:, :, 0, 1], C0[:, 0], n0[:, 0], m0[:, 0])
    hd_b, (C_b, n_b, m_b) = _mlstm_scan(_rev(q), _rev(k), _rev(v), _rev(gt[:, :, 1, 0]), _rev(gt[:, :, 1, 1]),
                                        C0[:, 1], n0[:, 1], m0[:, 1])
    out_d = jax.nn.sigmoid(o) * _head_norm(hd_f + _rev(hd_b), lp['ml_norm_w']).reshape(bn, L, W_ML)

    br = jnp.concatenate([out_a, out_b, out_c, out_d], axis=-1)
    states = (jnp.stack([hl_f, hl_b], axis=1), jnp.stack([S_f, S_b], axis=1), jnp.stack([C_f, C_b], axis=1),
              jnp.stack([n_f, n_b], axis=1), jnp.stack([m_f, m_b], axis=1))
    return br, states


def kernel(x_prompt, x_sample, state_rglru, state_ssd, state_mlstm_C, state_mlstm_n, state_mlstm_m, c, c_ctx,
           ada_w, ada_b, ln_g, ln_b, ffn_wg, ffn_wu, ffn_wd, w_in, rg_conv_w, rg_conv_b, rg_gate_w, rg_gate_b,
           rg_lambda, hy_conv_w, hy_conv_b, hy_w1, hy_b1, hy_w2, hy_b2, hy_w3, hy_decay, hy_bias,
           ssd_conv_w, ssd_conv_b, ssd_dt_bias, ssd_A_log, ssd_D, ssd_norm_w, ml_gate_b, ml_norm_w,
           branch_w, mix_out):
    bp, lp_len = x_prompt.shape[:2]
    bs, ls = x_sample.shape[:2]
    mp, ms = bp * lp_len, bs * ls
    m_tot = mp + ms

    def midx(i, bm):
        return jnp.where(i < mp // bm, 0, 1 + (i - mp // bm) // (ls // bm))

    n_rows = ls // GRID_W
    rows = jnp.repeat(jnp.arange(n_rows), GRID_W)
    cols = jnp.arange(n_rows * GRID_W) % GRID_W
    zero_st = (jnp.zeros((bp, 2, W_RG), F32), jnp.zeros((bp, 2, SSD_H, SSD_P, SSD_N), F32),
               jnp.zeros((bp, 2, ML_NH, ML_DH, ML_DH), F32), jnp.zeros((bp, 2, ML_NH, ML_DH), F32),
               jnp.zeros((bp, 2, ML_NH), F32))

    cond = jnp.concatenate([c_ctx[None], c, jnp.zeros((8 - 1 - bs, D_MODEL), F32)], axis=0)
    s = jax.nn.silu(cond).astype(BF16)
    mod = _ada_call(s, ada_w, ada_b.reshape(DEPTH, 1, N_MOD * D_MODEL))
    mod = mod[:, :1 + bs].reshape(DEPTH, 1 + bs, N_MOD, 1, D_MODEL)

    def mod_row(l, j):
        return mod[l, :, j]

    o = IN_OFFSETS
    w_small = jnp.concatenate(
        [w_in[:, :, :o[4]].astype(BF16),
         w_in[:, :, o[5]:o[6]].astype(BF16),
         w_in[:, :, o[4]:o[5]].astype(BF16),
         w_in[:, :, o[6]:o[7]].astype(BF16),
         jnp.zeros((DEPTH, D_MODEL, N_PROJ - O_SMALL - 2 * SSD_H - 4 * ML_NH), BF16)], axis=-1)
    w_merge = w_in[:, :, o[7]:].astype(BF16)
    wg_b, wu_b, wd_b = ffn_wg.astype(BF16), ffn_wu.astype(BF16), ffn_wd.astype(BF16)
    bw_b, mo_b = branch_w.astype(BF16), mix_out.astype(BF16)

    x = jnp.concatenate([x_prompt.reshape(mp, D_MODEL), x_sample.reshape(ms, D_MODEL)], axis=0)
    h = _modulate_call(x, mod_row(0, 0), mod_row(0, 1), midx)

    new_states = []
    for l in range(DEPTH):
        lp = dict(rg_conv_w=rg_conv_w[l], rg_conv_b=rg_conv_b[l], rg_gate_w=rg_gate_w[l], rg_gate_b=rg_gate_b[l],
                  rg_lambda=rg_lambda[l], hy_conv_w=hy_conv_w[l], hy_conv_b=hy_conv_b[l], hy_w1=hy_w1[l],
                  hy_b1=hy_b1[l], hy_w2=hy_w2[l], hy_b2=hy_b2[l], hy_w3=hy_w3[l], hy_decay=hy_decay[l],
                  hy_bias=hy_bias[l], ssd_conv_w=ssd_conv_w[l], ssd_conv_b=ssd_conv_b[l],
                  ssd_dt_bias=ssd_dt_bias[l], ssd_A_log=ssd_A_log[l], ssd_D=ssd_D[l], ssd_norm_w=ssd_norm_w[l],
                  ml_gate_b=ml_gate_b[l], ml_norm_w=ml_norm_w[l])
        lng = ln_g[l].reshape(3, 1, D_MODEL)
        lnb = ln_b[l].reshape(3, 1, D_MODEL)

        a = _ffn_up_call(h, wg_b[l, 0], wu_b[l, 0])
        x, h = _res_ln_call(a, wd_b[l, 0], x, mod_row(l, 2), lng[0], lnb[0], mod_row(l, 3), mod_row(l, 4),
                            midx, 0.5, True)
        proj = _in_proj_call(h, w_small[l])
        cache_l = (state_rglru[:, l], state_ssd[:, l], state_mlstm_C[:, l], state_mlstm_n[:, l], state_mlstm_m[:, l])
        br_p, st = _mixers_jax(proj[:mp].reshape(bp, lp_len, N_PROJ), zero_st, lp, None)
        br_s, _ = _mixers_jax(proj[mp:].reshape(bs, ls, N_PROJ), cache_l, lp, (rows, cols))
        new_states.append(st)
        br = jnp.concatenate([br_p.reshape(mp, D_MODEL), br_s.reshape(ms, D_MODEL)], axis=0).astype(BF16)
        merged = _merge_call(h, br, w_merge[l], bw_b[l])
        x, h = _res_ln_call(merged, mo_b[l], x, mod_row(l, 5), lng[1], lnb[1], mod_row(l, 6), mod_row(l, 7),
                            midx, 1.0, True)
        a = _ffn_up_call(h, wg_b[l, 1], wu_b[l, 1])
        last = l == DEPTH - 1
        nl = l if last else l + 1
        x, h = _res_ln_call(a, wd_b[l, 1], x, mod_row(l, 8), lng[2], lnb[2], mod_row(nl, 0), mod_row(nl, 1),
                            midx, 0.5, not last)

    yp = x[:mp].reshape(bp, lp_len, D_MODEL)
    ys = x[mp:].reshape(bs, ls, D_MODEL)
    outs = tuple(jnp.stack([new_states[l][j] for l in range(DEPTH)], axis=1) for j in range(5))
    return (yp, ys) + outs
```

```python
import functools
import math

import jax
import jax.numpy as jnp
import numpy as np
from jax import lax
from jax.experimental import pallas as pl
from jax.experimental.pallas import tpu as pltpu

D_MODEL = 4096
DEPTH = 2
GRID_W = 64
N_BRANCH = 4
W_BRANCH = D_MODEL // N_BRANCH
D_FF = 2 * D_MODEL
N_MOD = 9
CHUNK = 128
W_RG = W_BRANCH
RG_BLOCKS = 8
RG_BW = W_RG // RG_BLOCKS
RG_C = 8.0
W_HY = W_BRANCH
HY_ORDER = 2
HY_BANDS = 16
HY_EMB = 2 * HY_BANDS + 1
HY_FH = 64
SSD_DI = W_BRANCH
SSD_P = 64
SSD_H = SSD_DI // SSD_P
SSD_N = 128
SSD_G = 2
SSD_CONV_CH = SSD_DI + 2 * SSD_G * SSD_N
W_ML = W_BRANCH
ML_NH = 4
ML_DH = W_ML // ML_NH
ROPE_BASE = 10000.0
ALPHA = (2 * DEPTH) ** 0.25

IN_SIZES = (W_RG, W_RG, 3 * W_HY, SSD_DI, SSD_CONV_CH, 2 * SSD_H, 4 * W_ML, 4 * ML_NH, N_BRANCH * D_MODEL)
IN_OFFSETS = tuple(int(s) for s in np.cumsum(IN_SIZES)[:-1])

F32 = jnp.float32
BF16 = jnp.bfloat16
HI = lax.Precision.HIGHEST
NT = (((1,), (1,)), ((), ()))
TN = (((0,), (0,)), ((), ()))

O_HY = 0
O_RGX = O_HY + 3 * W_HY
O_RGY = O_RGX + W_RG
O_SSDZ = O_RGY + W_RG
O_XBC = O_SSDZ + SSD_DI
O_SMALL = O_XBC + SSD_CONV_CH
W_SMALL = 128
O_QKVO = 8192
N_PROJ = O_QKVO + 4 * W_ML

VMEM_LIMIT = 56 * 1024 * 1024


def _cparams(sem):
    return pltpu.CompilerParams(dimension_semantics=sem, vmem_limit_bytes=VMEM_LIMIT)


def _dot(a, b):
    return jnp.dot(a, b, preferred_element_type=F32)


def _dg(a, b, dims, precision=None):
    return lax.dot_general(a, b, dims, preferred_element_type=F32, precision=precision)


def _ada_body(s_ref, w_ref, b_ref, o_ref):
    o_ref[...] = _dot(s_ref[...], w_ref[...].astype(BF16)) + b_ref[...]


def _ada_call(s, ada_w, ada_b):
    bn = 512
    n = ada_w.shape[-1]
    return pl.pallas_call(
        _ada_body,
        grid=(DEPTH, n // bn),
        in_specs=[pl.BlockSpec((8, D_MODEL), lambda l, j: (0, 0)),
                  pl.BlockSpec((None, D_MODEL, bn), lambda l, j: (l, 0, j)),
                  pl.BlockSpec((None, 1, bn), lambda l, j: (l, 0, j))],
        out_specs=pl.BlockSpec((None, 8, bn), lambda l, j: (l, 0, j)),
        out_shape=jax.ShapeDtypeStruct((DEPTH, 8, n), F32),
        compiler_params=_cparams(("parallel", "parallel")),
        name="ada_mod",
    )(s, ada_w, ada_b)


def _modulate_body(x_ref, sh_ref, sc_ref, o_ref):
    o_ref[...] = (x_ref[...] * (1.0 + sc_ref[...]) + sh_ref[...]).astype(BF16)


def _modulate_call(x, shift, scale, midx):
    m = x.shape[0]
    bm = 256
    mspec = pl.BlockSpec((None, 1, D_MODEL), lambda i: (midx(i, bm), 0, 0))
    return pl.pallas_call(
        _modulate_body,
        grid=(m // bm,),
        in_specs=[pl.BlockSpec((bm, D_MODEL), lambda i: (i, 0)), mspec, mspec],
        out_specs=pl.BlockSpec((bm, D_MODEL), lambda i: (i, 0)),
        out_shape=jax.ShapeDtypeStruct((m, D_MODEL), BF16),
        compiler_params=_cparams(("parallel",)),
        name="modulate",
    )(x, shift, scale)


def _ffn_up_body(h_ref, wg_ref, wu_ref, o_ref):
    h = h_ref[...]
    g = _dot(h, wg_ref[...])
    u = _dot(h, wu_ref[...])
    o_ref[...] = (g * jax.nn.sigmoid(g) * u).astype(BF16)


def _ffn_up_call(h, wg, wu):
    m = h.shape[0]
    bm, bn = 1024, 512
    return pl.pallas_call(
        _ffn_up_body,
        grid=(m // bm, D_FF // bn),
        in_specs=[pl.BlockSpec((bm, D_MODEL), lambda i, j: (i, 0)),
                  pl.BlockSpec((D_MODEL, bn), lambda i, j: (0, j)),
                  pl.BlockSpec((D_MODEL, bn), lambda i, j: (0, j))],
        out_specs=pl.BlockSpec((bm, bn), lambda i, j: (i, j)),
        out_shape=jax.ShapeDtypeStruct((m, D_FF), BF16),
        compiler_params=_cparams(("parallel", "parallel")),
        name="ffn_up",
    )(h, wg, wu)


def _in_proj_body(h_ref, w_ref, o_ref):
    o_ref[...] = _dot(h_ref[...], w_ref[...])


def _in_proj_call(h, w):
    m = h.shape[0]
    bm, bn = 1024, 512
    return pl.pallas_call(
        _in_proj_body,
        grid=(m // bm, N_PROJ // bn),
        in_specs=[pl.BlockSpec((bm, D_MODEL), lambda i, j: (i, 0)),
                  pl.BlockSpec((D_MODEL, bn), lambda i, j: (0, j))],
        out_specs=pl.BlockSpec((bm, bn), lambda i, j: (i, j)),
        out_shape=jax.ShapeDtypeStruct((m, N_PROJ), F32),
        compiler_params=_cparams(("parallel", "parallel")),
        name="in_proj",
    )(h, w)


def _merge_body(h_ref, b0, b1, b2, b3, wg0, wg1, wg2, wg3, wb0, wb1, wb2, wb3, o_ref):
    h = h_ref[...]
    acc = None
    for br, wg, wb in ((b0, wg0, wb0), (b1, wg1, wb1), (b2, wg2, wb2), (b3, wg3, wb3)):
        t = jax.nn.sigmoid(_dot(h, wg[...])) * _dot(br[...], wb[...])
        acc = t if acc is None else acc + t
    o_ref[...] = acc.astype(BF16)


def _merge_call(h, branches, w_merge, branch_w):
    m = h.shape[0]
    bm, bn = 512, 256
    nb = D_MODEL // bn
    br_specs = [pl.BlockSpec((bm, W_BRANCH), lambda i, n: (i, 0)) for _ in range(N_BRANCH)]
    wg_specs = [pl.BlockSpec((D_MODEL, bn), functools.partial(lambda i, n, j: (0, j * nb + n), j=j))
                for j in range(N_BRANCH)]
    wb_specs = [pl.BlockSpec((None, W_BRANCH, bn), functools.partial(lambda i, n, j: (j, 0, n), j=j))
                for j in range(N_BRANCH)]
    return pl.pallas_call(
        _merge_body,
        grid=(m // bm, nb),
        in_specs=[pl.BlockSpec((bm, D_MODEL), lambda i, n: (i, 0))] + br_specs + wg_specs + wb_specs,
        out_specs=pl.BlockSpec((bm, bn), lambda i, n: (i, n)),
        out_shape=jax.ShapeDtypeStruct((m, D_MODEL), BF16),
        compiler_params=_cparams(("parallel", "parallel")),
        name="merge",
    )(h, *branches, *([w_merge] * N_BRANCH), *([branch_w] * N_BRANCH))


def _res_ln_body(a_ref, w_ref, x_ref, gate_ref, lng_ref, lnb_ref, nsh_ref, nsc_ref, y_ref, hn_ref, st_ref, *,
                 coef, nk, bm, emit_next):
    k = pl.program_id(1)

    @pl.when(k == 0)
    def _():
        y_ref[...] = _dot(a_ref[...], w_ref[...])

    @pl.when(k > 0)
    def _():
        y_ref[...] += _dot(a_ref[...], w_ref[...])

    @pl.when(k == nk - 1)
    def _():
        rows = 8
        gate = coef * gate_ref[...]
        lng, lnb = lng_ref[...], lnb_ref[...]
        nsc, nsh = 1.0 + nsc_ref[...], nsh_ref[...]

        def rows_of(c):
            return pl.ds(pl.multiple_of(c * rows, rows), rows)

        def sweep_mean(c, carry):
            rs = rows_of(c)
            r = ALPHA * x_ref[rs, :] + gate * y_ref[rs, :]
            y_ref[rs, :] = r
            st_ref[rs, 0:1] = jnp.mean(r, axis=-1, keepdims=True)
            return carry

        def sweep_var(c, carry):
            rs = rows_of(c)
            d = y_ref[rs, :] - st_ref[rs, 0:1]
            st_ref[rs, 1:2] = lax.rsqrt(jnp.mean(d * d, axis=-1, keepdims=True) + 1e-5)
            return carry

        def sweep_out(c, carry):
            rs = rows_of(c)
            o = (y_ref[rs, :] - st_ref[rs, 0:1]) * st_ref[rs, 1:2] * lng + lnb
            y_ref[rs, :] = o
            if emit_next:
                hn_ref[rs, :] = (o * nsc + nsh).astype(BF16)
            return carry

        for sweep, unroll in ((sweep_mean, 8), (sweep_var, 8), (sweep_out, 4)):
            lax.fori_loop(0, bm // rows, sweep, 0, unroll=unroll)
        if not emit_next:
            hn_ref[...] = jnp.zeros_like(hn_ref)


def _res_ln_call(a, w, x, gate, lng, lnb, nshift, nscale, midx, coef, emit_next):
    m, kdim = a.shape
    bm, bk = 512, 512
    nk = kdim // bk
    hn_rows = bm if emit_next else 8
    mspec = pl.BlockSpec((None, 1, D_MODEL), lambda i, k: (midx(i, bm), 0, 0))
    vspec = pl.BlockSpec((1, D_MODEL), lambda i, k: (0, 0))
    y, hn = pl.pallas_call(
        functools.partial(_res_ln_body, coef=coef, nk=nk, bm=bm, emit_next=emit_next),
        grid=(m // bm, nk),
        in_specs=[pl.BlockSpec((bm, bk), lambda i, k: (i, k)),
                  pl.BlockSpec((bk, D_MODEL), lambda i, k: (k, 0)),
                  pl.BlockSpec((bm, D_MODEL), lambda i, k: (i, 0)),
                  mspec, vspec, vspec, mspec, mspec],
        out_specs=[pl.BlockSpec((bm, D_MODEL), lambda i, k: (i, 0)),
                   pl.BlockSpec((hn_rows, D_MODEL), lambda i, k: (i, 0))],
        out_shape=[jax.ShapeDtypeStruct((m, D_MODEL), F32),
                   jax.ShapeDtypeStruct((m if emit_next else 8 * (m // bm), D_MODEL), BF16)],
        scratch_shapes=[pltpu.VMEM((bm, 128), F32)],
        compiler_params=_cparams(("parallel", "arbitrary")),
        name="res_ln",
    )(a, w, x, gate, lng, lnb, nshift, nscale)
    return y, hn


def _conv_piece(pad_ref, w_ref, b_ref, o_ref, c0, width, taps, pad_l, act):
    cs = slice(c0, c0 + width)
    acc = b_ref[:, cs]
    for j in range(taps):
        r0 = 8 - pad_l + j
        acc = acc + w_ref[j:j + 1, cs] * pad_ref[r0:r0 + CHUNK, cs]
    o_ref[:, cs] = act(acc)


def _conv_body(hy_p, hy_c, hy_n, rg_p, rg_c, rg_n, xb_p, xb_c, xb_n,
               hw_ref, hb_ref, rw_ref, rb_ref, xw_ref, xb_ref,
               hu_ref, xa_ref, xbc_ref, pad_ref, *, chunk_pos):
    first, last = chunk_pos(pl.program_id(0))
    for prev, cur, nxt, w_ref, b_ref, o_ref, width, taps, pad_l, act in (
            (hy_p, hy_c, hy_n, hw_ref, hb_ref, hu_ref, 3 * W_HY, 3, 1, lambda v: v),
            (rg_p, rg_c, rg_n, rw_ref, rb_ref, xa_ref, W_RG, 4, 2, lambda v: v),
            (xb_p, xb_c, xb_n, xw_ref, xb_ref, xbc_ref, SSD_CONV_CH, 4, 2, jax.nn.silu)):
        pad_ref[0:8, 0:width] = jnp.where(first, 0.0, prev[...])
        pad_ref[8:8 + CHUNK, 0:width] = cur[...]
        pad_ref[8 + CHUNK:16 + CHUNK, 0:width] = jnp.where(last, 0.0, nxt[...])
        for c0 in range(0, width, 512):
            _conv_piece(pad_ref, w_ref, b_ref, o_ref, c0, min(512, width - c0), taps, pad_l, act)


def _conv_call(proj, hy_w, hy_b, rg_w, rg_b, xb_w, xb_b, mp_chunks, ncp, ncs):
    m = proj.shape[0]
    n8 = m // 8

    def chunk_pos(i):
        pos = jnp.where(i < mp_chunks, i % ncp, (i - mp_chunks) % ncs)
        n = jnp.where(i < mp_chunks, ncp, ncs)
        return pos == 0, pos == n - 1

    def trio(width, col):
        return [pl.BlockSpec((8, width), lambda i: (jnp.maximum(i * (CHUNK // 8) - 1, 0), col)),
                pl.BlockSpec((CHUNK, width), lambda i: (i, col)),
                pl.BlockSpec((8, width), lambda i: (jnp.minimum((i + 1) * (CHUNK // 8), n8 - 1), col))]

    def full(a):
        return pl.BlockSpec(a.shape, lambda i: (0, 0))

    params = (hy_w, hy_b, rg_w, rg_b, xb_w, xb_b)
    return pl.pallas_call(
        functools.partial(_conv_body, chunk_pos=chunk_pos),
        grid=(m // CHUNK,),
        in_specs=trio(3 * W_HY, O_HY // (3 * W_HY)) + trio(W_RG, O_RGX // W_RG)
        + trio(SSD_CONV_CH, O_XBC // SSD_CONV_CH) + [full(a) for a in params],
        out_specs=[pl.BlockSpec((CHUNK, 3 * W_HY), lambda i: (i, 0)),
                   pl.BlockSpec((CHUNK, W_RG), lambda i: (i, 0)),
                   pl.BlockSpec((CHUNK, SSD_CONV_CH), lambda i: (i, 0))],
        out_shape=[jax.ShapeDtypeStruct((m, 3 * W_HY), F32),
                   jax.ShapeDtypeStruct((m, W_RG), F32),
                   jax.ShapeDtypeStruct((m, SSD_CONV_CH), F32)],
        scratch_shapes=[pltpu.VMEM((CHUNK + 16, 3 * W_HY), F32)],
        compiler_params=_cparams(("parallel",)),
        name="dw_conv",
    )(*([proj] * 9), *params)


def _chunk_of(j, nc):
    return jnp.where(j < nc, j, 2 * nc - 1 - j)


def _row_map(base, nc, col):
    return lambda b, j: (base + b * nc + _chunk_of(j, nc), col)


def _out_map(nc):
    return lambda b, j: (b * nc + jnp.where(j < nc, nc - 1, 2 * nc - 1 - j), 0)


def _tri(rev):
    r = lax.broadcasted_iota(jnp.int32, (CHUNK, CHUNK), 0)
    c = lax.broadcasted_iota(jnp.int32, (CHUNK, CHUNK), 1)
    return (c >= r) if rev else (c <= r)


def _two_phase(nc, run):
    j = pl.program_id(1)

    @pl.when(j < nc)
    def _():
        run(0, j == 0, j == nc - 1, j)

    @pl.when(j >= nc)
    def _():
        run(1, j == nc, j == 2 * nc - 1, 2 * nc - 1 - j)


def _rglru_body(xa_ref, y_ref, gw_ref, gb_ref, lam_ref, h0_ref, o_ref, hl_ref,
                yf_ref, a_ref, u_ref, hb_ref, hc_ref, *, nc, init):
    def run(d, is_first, is_last, ci):
        x = xa_ref[...]
        xb = x.astype(BF16)
        sp = jax.nn.softplus(-lam_ref[d:d + 1, :])
        for n in range(RG_BLOCKS):
            cs = slice(n * RG_BW, (n + 1) * RG_BW)
            r = jax.nn.sigmoid(_dot(xb[:, cs], gw_ref[d, 0, n]) + gb_ref[2 * d:2 * d + 1, cs])
            i = jax.nn.sigmoid(_dot(xb[:, cs], gw_ref[d, 1, n]) + gb_ref[2 * d + 1:2 * d + 2, cs])
            log_a = -RG_C * r * sp[:, cs]
            th = jnp.tanh(log_a)
            a_ref[:, cs] = jnp.exp(log_a)
            u_ref[:, cs] = jnp.sqrt(-2.0 * th / (1.0 - th)) * (i * x[:, cs])

        @pl.when(is_first)
        def _():
            hc_ref[0:1, :] = h0_ref[d:d + 1, :] if init else jnp.zeros((1, W_RG), F32)

        def step(s, h):
            t = (CHUNK - 1 - s) if d else s
            h = a_ref[pl.ds(t, 1), :] * h + u_ref[pl.ds(t, 1), :]
            hb_ref[pl.ds(t, 1), :] = h
            return h

        h = lax.fori_loop(0, CHUNK, step, hc_ref[0:1, :], unroll=8)
        hc_ref[0:1, :] = h

        @pl.when(is_last)
        def _():
            hl_ref[d:d + 1, :] = h

        r0 = pl.multiple_of(ci * CHUNK, CHUNK)
        if d == 0:
            yf_ref[pl.ds(r0, CHUNK), :] = hb_ref[...]
        else:
            o_ref[...] = ((yf_ref[pl.ds(r0, CHUNK), :] + hb_ref[...]) * jax.nn.gelu(y_ref[...])).astype(BF16)

    _two_phase(nc, run)


def _rglru_call(xa, proj, gw, gb, lam, h0, base, n_seq, nc):
    init = h0 is not None
    if not init:
        h0 = jnp.zeros((n_seq, 2, W_RG), F32)
    L = nc * CHUNK
    st_spec = pl.BlockSpec((None, 2, W_RG), lambda b, j: (b, 0, 0))
    return pl.pallas_call(
        functools.partial(_rglru_body, nc=nc, init=init),
        grid=(n_seq, 2 * nc),
        in_specs=[pl.BlockSpec((CHUNK, W_RG), _row_map(base, nc, 0)),
                  pl.BlockSpec((CHUNK, W_RG), _row_map(base, nc, O_RGY // W_RG)),
                  pl.BlockSpec(gw.shape, lambda b, j: (0, 0, 0, 0, 0)),
                  pl.BlockSpec(gb.shape, lambda b, j: (0, 0)),
                  pl.BlockSpec(lam.shape, lambda b, j: (0, 0)),
                  st_spec],
        out_specs=[pl.BlockSpec((CHUNK, W_RG), _out_map(nc)), st_spec],
        out_shape=[jax.ShapeDtypeStruct((n_seq * L, W_RG), BF16),
                   jax.ShapeDtypeStruct((n_seq, 2, W_RG), F32)],
        scratch_shapes=[pltpu.VMEM((L, W_RG), F32), pltpu.VMEM((CHUNK, W_RG), F32), pltpu.VMEM((CHUNK, W_RG), F32),
                        pltpu.VMEM((CHUNK, W_RG), F32), pltpu.VMEM((8, W_RG), F32)],
        compiler_params=_cparams(("parallel", "arbitrary")),
        name="rglru",
    )(xa, proj, gw, gb, lam, h0)


def _ssd_body(xbc_ref, z_ref, sm_ref, dtb_r, dtb_c, al_r, al_c, dvec_ref, nw_ref, s0_ref, o_ref, sl_ref,
              yf_ref, yb_ref, s_ref, *, nc, init):
    def run(d, is_first, is_last, ci):
        hs = slice(SSD_H * d, SSD_H * (d + 1))
        sm = sm_ref[...]
        smt = sm.T
        dt_c = jax.nn.softplus(sm[:, hs] + dtb_r[:, hs])
        dt_r = jax.nn.softplus(smt[hs, :] + dtb_c[hs, :])
        mask = _tri(d)
        tri = jnp.where(mask, 1.0, 0.0).astype(F32)
        cs_c = jnp.dot(tri, dt_c * -jnp.exp(al_r[:, hs]), preferred_element_type=F32, precision=HI)
        cs_r = _dg(dt_r * -jnp.exp(al_c[hs, :]), tri, NT, HI)
        e = 0 if d else CHUNK - 1
        tot = cs_c[e:e + 1, :]
        w_end = jnp.exp(tot - cs_c) * dt_c
        ecs = jnp.exp(cs_c)
        etot = jnp.exp(tot)

        @pl.when(is_first)
        def _():
            s_ref[...] = s0_ref[d] if init else jnp.zeros(s_ref.shape, F32)

        xs = xbc_ref[:, 0:SSD_DI]
        xsb = xs.astype(BF16)
        for g in range(SSD_G):
            bb = xbc_ref[:, SSD_DI + g * SSD_N:SSD_DI + (g + 1) * SSD_N].astype(BF16)
            cb = xbc_ref[:, SSD_DI + (SSD_G + g) * SSD_N:SSD_DI + (SSD_G + g + 1) * SSD_N].astype(BF16)
            gmat = _dg(cb, bb, NT)
            for hh in range(SSD_H // SSD_G):
                h = g * (SSD_H // SSD_G) + hh
                ps = slice(h * SSD_P, (h + 1) * SSD_P)
                seg = cs_c[:, h:h + 1] - cs_r[h:h + 1, :]
                dec = jnp.exp(jnp.where(mask, seg, -jnp.inf))
                mh = (gmat * dec * dt_r[h:h + 1, :]).astype(BF16)
                sh = s_ref[h]
                yb_ref[:, ps] = _dot(mh, xsb[:, ps]) + _dg(cb, sh.astype(BF16), NT) * ecs[:, h:h + 1]
                xw = (xs[:, ps] * w_end[:, h:h + 1]).astype(BF16)
                s_ref[h] = etot[:, h:h + 1] * sh + _dg(xw, bb, TN)

        @pl.when(is_last)
        def _():
            sl_ref[d] = s_ref[...]

        r0 = pl.multiple_of(ci * CHUNK, CHUNK)
        if d == 0:
            yf_ref[pl.ds(r0, CHUNK), :] = yb_ref[...]
        else:
            yc = yf_ref[pl.ds(r0, CHUNK), :] + yb_ref[...] + dvec_ref[...] * xs
            v = yc * jax.nn.silu(z_ref[...])
            o_ref[...] = (v * lax.rsqrt(jnp.mean(v * v, axis=-1, keepdims=True) + 1e-6) * nw_ref[...]).astype(BF16)

    _two_phase(nc, run)


def _ssd_call(xbc, proj, dt_bias, a_log, dvec, norm_w, s0, base, n_seq, nc):
    init = s0 is not None
    if not init:
        s0 = jnp.zeros((n_seq, 2, 8, 8, SSD_N), F32)
    L = nc * CHUNK
    st_shape = (SSD_H, SSD_P, SSD_N)
    dtb_r, al_r = dt_bias.reshape(1, 2 * SSD_H), a_log.reshape(1, 2 * SSD_H)
    dtb_c, al_c = dt_bias.reshape(2 * SSD_H, 1), a_log.reshape(2 * SSD_H, 1)

    def full(a):
        return pl.BlockSpec(a.shape, lambda b, j: (0,) * a.ndim)

    s0_spec = pl.BlockSpec((None,) + s0.shape[1:], lambda b, j: (b, 0, 0, 0, 0))
    return pl.pallas_call(
        functools.partial(_ssd_body, nc=nc, init=init),
        grid=(n_seq, 2 * nc),
        in_specs=[pl.BlockSpec((CHUNK, SSD_CONV_CH), _row_map(base, nc, 0)),
                  pl.BlockSpec((CHUNK, SSD_DI), _row_map(base, nc, O_SSDZ // SSD_DI)),
                  pl.BlockSpec((CHUNK, W_SMALL), _row_map(base, nc, O_SMALL // W_SMALL)),
                  full(dtb_r), full(dtb_c), full(al_r), full(al_c), full(dvec), full(norm_w), s0_spec],
        out_specs=[pl.BlockSpec((CHUNK, SSD_DI), _out_map(nc)),
                   pl.BlockSpec((None, 2) + st_shape, lambda b, j: (b, 0, 0, 0, 0))],
        out_shape=[jax.ShapeDtypeStruct((n_seq * L, SSD_DI), BF16),
                   jax.ShapeDtypeStruct((n_seq, 2) + st_shape, F32)],
        scratch_shapes=[pltpu.VMEM((L, SSD_DI), F32), pltpu.VMEM((CHUNK, SSD_DI), F32), pltpu.VMEM(st_shape, F32)],
        compiler_params=_cparams(("parallel", "arbitrary")),
        name="ssd",
    )(xbc, proj, proj, dtb_r, dtb_c, al_r, al_c, dvec, norm_w, s0)


def _mlstm_body(qkvo_ref, sm_ref, gb_r, gb_c, nw_ref, cos_ref, sin_ref, c0_ref, n0_ref, m0_ref,
                o_ref, cl_ref, nl_ref, ml_ref, hf_ref, hb_ref, c_ref, n_ref, m_ref, *, nc, init, rope):
    def run(d, is_first, is_last, ci):
        sm = sm_ref[...]
        smt = sm.T
        o_i, o_f = 2 * SSD_H + 8 * d, 2 * SSD_H + 8 * d + ML_NH
        i_c = sm[:, o_i:o_i + ML_NH] + gb_r[:, 8 * d:8 * d + ML_NH]
        i_r = smt[o_i:o_i + ML_NH, :] + gb_c[8 * d:8 * d + ML_NH, :]
        lf_c = jax.nn.log_sigmoid(sm[:, o_f:o_f + ML_NH] + gb_r[:, 8 * d + ML_NH:8 * d + 2 * ML_NH])
        lf_r = jax.nn.log_sigmoid(smt[o_f:o_f + ML_NH, :] + gb_c[8 * d + ML_NH:8 * d + 2 * ML_NH, :])
        mask = _tri(d)
        tri = jnp.where(mask, 1.0, 0.0).astype(F32)
        b_c = jnp.dot(tri, lf_c, preferred_element_type=F32, precision=HI)
        b_r = _dg(lf_r, tri, NT, HI)
        e = 0 if d else CHUNK - 1
        b_end = b_c[e:e + 1, :]

        @pl.when(is_first)
        def _():
            if init:
                c_ref[...] = c0_ref[d]
                n_ref[0:ML_NH, :] = n0_ref[d]
                m_ref[0:1, 0:ML_NH] = m0_ref[d:d + 1, :]
            else:
                c_ref[...] = jnp.zeros(c_ref.shape, F32)
                n_ref[...] = jnp.zeros(n_ref.shape, F32)
                m_ref[...] = jnp.zeros(m_ref.shape, F32)

        def rot(t):
            if not rope:
                return t
            parts = []
            for p in range(ML_DH // 128):
                tp = t[:, p * 128:(p + 1) * 128]
                parts.append(tp * cos_ref[:, p * 128:(p + 1) * 128]
                             + pltpu.roll(tp, 64, axis=1) * sin_ref[:, p * 128:(p + 1) * 128])
            return jnp.concatenate(parts, axis=-1)

        for h in range(ML_NH):
            hsl = slice(h * ML_DH, (h + 1) * ML_DH)
            q = rot(qkvo_ref[:, h * ML_DH:(h + 1) * ML_DH])
            k = rot(qkvo_ref[:, W_ML + h * ML_DH:W_ML + (h + 1) * ML_DH] * (ML_DH ** -0.5))
            v = qkvo_ref[:, 2 * W_ML + h * ML_DH:2 * W_ML + (h + 1) * ML_DH]
            qb, kb = q.astype(BF16), k.astype(BF16)
            m_prev = m_ref[0:1, h:h + 1]
            dmat = jnp.where(mask, b_c[:, h:h + 1] - b_r[h:h + 1, :] + i_r[h:h + 1, :], -jnp.inf)
            g0 = b_c[:, h:h + 1] + m_prev
            mt = jnp.maximum(g0, jnp.max(dmat, axis=1, keepdims=True))
            w = jnp.exp(dmat - mt)
            w0 = jnp.exp(g0 - mt)
            s = _dg(qb, kb, NT) * w
            ch = c_ref[h]
            nh = n_ref[h:h + 1, :]
            num = _dot(s.astype(BF16), v.astype(BF16)) + w0 * _dg(qb, ch.astype(BF16), NT)
            den = jnp.sum(s, axis=1, keepdims=True) + w0 * jnp.sum(q * nh, axis=1, keepdims=True)
            hb_ref[:, hsl] = num / jnp.maximum(jnp.abs(den), jnp.exp(-mt))
            d_end = b_end[:, h:h + 1] - b_c[:, h:h + 1] + i_c[:, h:h + 1]
            g0e = b_end[:, h:h + 1] + m_prev
            m_new = jnp.maximum(g0e, jnp.max(d_end, axis=0, keepdims=True))
            we = jnp.exp(d_end - m_new)
            w0e = jnp.exp(g0e - m_new)
            c_ref[h] = w0e * ch + _dg((v * we).astype(BF16), kb, TN)
            n_ref[h:h + 1, :] = w0e * nh + jnp.sum(we * k, axis=0, keepdims=True)
            m_ref[0:1, h:h + 1] = m_new

        @pl.when(is_last)
        def _():
            cl_ref[d] = c_ref[...]
            nl_ref[d] = n_ref[0:ML_NH, :]
            ml_ref[d:d + 1, :] = m_ref[0:1, 0:ML_NH]

        r0 = pl.multiple_of(ci * CHUNK, CHUNK)
        if d == 0:
            hf_ref[pl.ds(r0, CHUNK), :] = hb_ref[...]
        else:
            for h in range(ML_NH):
                hsl = slice(h * ML_DH, (h + 1) * ML_DH)
                t = hf_ref[pl.ds(r0, CHUNK), hsl] + hb_ref[:, hsl]
                mu = jnp.mean(t, axis=-1, keepdims=True)
                var = jnp.mean(jnp.square(t - mu), axis=-1, keepdims=True)
                og = jax.nn.sigmoid(qkvo_ref[:, 3 * W_ML + h * ML_DH:3 * W_ML + (h + 1) * ML_DH])
                o_ref[:, hsl] = (og * ((t - mu) * lax.rsqrt(var + 1e-5) * nw_ref[:, hsl])).astype(BF16)

    _two_phase(nc, run)


def _mlstm_call(proj, gate_b, norm_w, rope_tabs, st0, base, n_seq, nc):
    init = st0 is not None
    rope = rope_tabs is not None
    L = nc * CHUNK
    if init:
        c0, n0, m0 = st0
    else:
        c0, n0, m0 = jnp.zeros((n_seq, 2, 1, 8, 128), F32), jnp.zeros((n_seq, 2, 8, 128), F32), jnp.zeros((n_seq, 2, ML_NH), F32)
    if rope:
        cos_t, sin_t = rope_tabs
        tab_spec = pl.BlockSpec((CHUNK, ML_DH), lambda b, j: (_chunk_of(j, nc), 0))
    else:
        cos_t = sin_t = jnp.zeros((8, 128), F32)
        tab_spec = pl.BlockSpec((8, 128), lambda b, j: (0, 0))
    gb_r, gb_c = gate_b.reshape(1, 4 * ML_NH), gate_b.reshape(4 * ML_NH, 1)

    def full(a):
        return pl.BlockSpec(a.shape, lambda b, j: (0,) * a.ndim)

    def per_seq(a):
        return pl.BlockSpec((None,) + a.shape[1:], lambda b, j: (b,) + (0,) * (a.ndim - 1))

    c_shape, n_shape = (ML_NH, ML_DH, ML_DH), (ML_NH, ML_DH)
    return pl.pallas_call(
        functools.partial(_mlstm_body, nc=nc, init=init, rope=rope),
        grid=(n_seq, 2 * nc),
        in_specs=[pl.BlockSpec((CHUNK, 4 * W_ML), _row_map(base, nc, O_QKVO // (4 * W_ML))),
                  pl.BlockSpec((CHUNK, W_SMALL), _row_map(base, nc, O_SMALL // W_SMALL)),
                  full(gb_r), full(gb_c), full(norm_w), tab_spec, tab_spec,
                  per_seq(c0), per_seq(n0), per_seq(m0)],
        out_specs=[pl.BlockSpec((CHUNK, W_ML), _out_map(nc)),
                   pl.BlockSpec((None, 2) + c_shape, lambda b, j: (b, 0, 0, 0, 0)),
                   pl.BlockSpec((None, 2) + n_shape, lambda b, j: (b, 0, 0, 0)),
                   pl.BlockSpec((None, 2, ML_NH), lambda b, j: (b, 0, 0))],
        out_shape=[jax.ShapeDtypeStruct((n_seq * L, W_ML), BF16),
                   jax.ShapeDtypeStruct((n_seq, 2) + c_shape, F32),
                   jax.ShapeDtypeStruct((n_seq, 2) + n_shape, F32),
                   jax.ShapeDtypeStruct((n_seq, 2, ML_NH), F32)],
        scratch_shapes=[pltpu.VMEM((L, W_ML), F32), pltpu.VMEM((CHUNK, W_ML), F32), pltpu.VMEM(c_shape, F32),
                        pltpu.VMEM((8, ML_DH), F32), pltpu.VMEM((8, 128), F32)],
        compiler_params=_cparams(("parallel", "arbitrary")),
        name="mlstm",
    )(proj, proj, gb_r, gb_c, norm_w, cos_t, sin_t, c0, n0, m0)


def _hy_filt_body(ef_ref, eb_ref, tf_ref, tb_ref, w1_ref, b1_ref, w2_ref, b2_ref, w3f_ref, w3b_ref,
                  dcf_ref, dcb_ref, fa_ref, fb_ref, nyq_ref, *, L):
    def mlp(e_ref, t_ref, w3_ref, dc_ref):
        f = jnp.sin(_dot(e_ref[...], w1_ref[...]) + b1_ref[...])
        f = jnp.sin(_dot(f.astype(BF16), w2_ref[...]) + b2_ref[...])
        return _dot(f.astype(BF16), w3_ref[...]) * jnp.exp(-dc_ref[...] * t_ref[...])

    hf = mlp(ef_ref, tf_ref, w3f_ref, dcf_ref)
    row = lax.broadcasted_iota(jnp.int32, hf.shape, 0)
    hb = jnp.where(row == 0, 0.0, mlp(eb_ref, tb_ref, w3b_ref, dcb_ref))
    norm = jnp.sum(jnp.abs(hf), axis=0, keepdims=True) + jnp.sum(jnp.abs(hb), axis=0, keepdims=True) + 1e-6
    hf, hb = hf / norm, hb / norm
    sgn = (1 - 2 * (row & 1)).astype(F32)
    nyq_ref[...] = jnp.sum(sgn * (hf + hb), axis=0, keepdims=True)
    fa_ref[...] = hf.astype(BF16)
    fb_ref[...] = hb.astype(BF16)


def _hy_filt_call(L, emb_f, emb_b, t_f, t_b, w1, b1, w2, b2, w3, decay):
    tc = 256
    nct = W_HY // tc

    def full(a):
        return pl.BlockSpec(a.shape, lambda o, c: (0, 0))

    def col(dirn):
        return lambda o, c: (0, (2 * o + dirn) * nct + c)

    out_spec = pl.BlockSpec((L, tc), lambda o, c: (0, o * nct + c))
    return pl.pallas_call(
        functools.partial(_hy_filt_body, L=L),
        grid=(HY_ORDER, nct),
        in_specs=[full(emb_f), full(emb_b), full(t_f), full(t_b), full(w1), full(b1), full(w2), full(b2),
                  pl.BlockSpec((HY_FH, tc), col(0)), pl.BlockSpec((HY_FH, tc), col(1)),
                  pl.BlockSpec((1, tc), col(0)), pl.BlockSpec((1, tc), col(1))],
        out_specs=[out_spec, out_spec, pl.BlockSpec((1, tc), lambda o, c: (0, o * nct + c))],
        out_shape=[jax.ShapeDtypeStruct((L, HY_ORDER * W_HY), BF16), jax.ShapeDtypeStruct((L, HY_ORDER * W_HY), BF16),
                   jax.ShapeDtypeStruct((1, HY_ORDER * W_HY), F32)],
        compiler_params=_cparams(("parallel", "parallel")),
        name="hy_filt",
    )(emb_f, emb_b, t_f, t_b, w1, b1, w2, b2, w3, w3, decay, decay)


def _hy_spec_body(fa_ref, fb_ref, cz_ref, sz_ref, fre_ref, fim_ref, *, kb):
    cz, sz = cz_ref[...], sz_ref[...]
    a, b = fa_ref[...], fb_ref[...]
    kk = pl.program_id(1) * kb + lax.broadcasted_iota(jnp.int32, (kb, 1), 0)
    sgn = (1 - 2 * (kk & 1)).astype(F32)
    fre_ref[...] = (_dot(cz, a) + sgn * _dot(cz, b)).astype(BF16)
    fim_ref[...] = (-(_dot(sz, a) + sgn * _dot(sz, b))).astype(BF16)


def _hy_spec_call(L, kb, fa, fb, cz, sz):
    tc = 256
    n = fa.shape[1]
    return pl.pallas_call(
        functools.partial(_hy_spec_body, kb=kb),
        grid=(n // tc, L // kb),
        in_specs=[pl.BlockSpec((L, tc), lambda c, k: (0, c)), pl.BlockSpec((L, tc), lambda c, k: (0, c)),
                  pl.BlockSpec((kb, L), lambda c, k: (k, 0)), pl.BlockSpec((kb, L), lambda c, k: (k, 0))],
        out_specs=[pl.BlockSpec((kb, tc), lambda c, k: (k, c)), pl.BlockSpec((kb, tc), lambda c, k: (k, c))],
        out_shape=[jax.ShapeDtypeStruct((L, n), BF16), jax.ShapeDtypeStruct((L, n), BF16)],
        compiler_params=_cparams(("parallel", "parallel")),
        name="hy_spec",
    )(fa, fb, cz, sz)


def _hy_conv_body(v_ref, x1_ref, x2_ref, fre_ref, fim_ref, nyq_ref, bias_ref, cz_ref, sz_ref, czt_ref, szt_ref,
                  o_ref, z_ref, zb_ref, acc_ref, *, L, kb, nkb):
    o, k = pl.program_id(2), pl.program_id(3)
    inv_n = 1.0 / (2 * L)

    @pl.when((o == 0) & (k == 0))
    def _():
        z_ref[...] = v_ref[...]

    @pl.when(k == 0)
    def _():
        zb_ref[...] = z_ref[...].astype(BF16)
        acc_ref[...] = jnp.zeros_like(acc_ref)

    zb = zb_ref[...]
    zc = _dot(cz_ref[...], zb)
    zs = _dot(sz_ref[...], zb)
    fre, fim = fre_ref[...].astype(F32), fim_ref[...].astype(F32)
    kk = k * kb + lax.broadcasted_iota(jnp.int32, (kb, 1), 0)
    wk = jnp.where(kk == 0, inv_n, 2.0 * inv_n)
    yre = ((zc * fre + zs * fim) * wk).astype(BF16)
    yim = ((zc * fim - zs * fre) * wk).astype(BF16)
    acc_ref[...] += _dot(czt_ref[...], yre) - _dot(szt_ref[...], yim)

    @pl.when(k == nkb - 1)
    def _():
        z = z_ref[...]
        row = lax.broadcasted_iota(jnp.int32, (L, 1), 0)
        sgn = (1 - 2 * (row & 1)).astype(F32)
        z_nyq = jnp.sum(z * sgn, axis=0, keepdims=True)
        y = acc_ref[...] + sgn * (z_nyq * nyq_ref[...] * inv_n) + bias_ref[...] * z

        @pl.when(o == 0)
        def _():
            z_ref[...] = x1_ref[...] * y

        @pl.when(o == 1)
        def _():
            o_ref[...] = (x2_ref[...] * y).astype(BF16)


def _hy_conv_call(hu, fre, fim, nyq, bias, cz, sz, L, kb, tc, base, n_seq):
    nct = W_HY // tc
    nkb = L // kb

    def zin(part):
        return pl.BlockSpec((L, tc), lambda b, c, o, k: (base + b, part * nct + c))

    fspec = pl.BlockSpec((kb, tc), lambda b, c, o, k: (k, o * nct + c))
    vspec = pl.BlockSpec((1, tc), lambda b, c, o, k: (0, o * nct + c))
    return pl.pallas_call(
        functools.partial(_hy_conv_body, L=L, kb=kb, nkb=nkb),
        grid=(n_seq, nct, HY_ORDER, nkb),
        in_specs=[zin(0), zin(1), zin(2), fspec, fspec, vspec, vspec,
                  pl.BlockSpec((kb, L), lambda b, c, o, k: (k, 0)), pl.BlockSpec((kb, L), lambda b, c, o, k: (k, 0)),
                  pl.BlockSpec((L, kb), lambda b, c, o, k: (0, k)), pl.BlockSpec((L, kb), lambda b, c, o, k: (0, k))],
        out_specs=pl.BlockSpec((L, tc), lambda b, c, o, k: (b, c)),
        out_shape=jax.ShapeDtypeStruct((n_seq * L, W_HY), BF16),
        scratch_shapes=[pltpu.VMEM((L, tc), F32), pltpu.VMEM((L, tc), BF16), pltpu.VMEM((L, tc), F32)],
        compiler_params=_cparams(("parallel", "parallel", "arbitrary", "arbitrary")),
        name="hy_conv",
    )(hu, hu, hu, fre, fim, nyq, bias, cz, sz, cz, sz)


def _hyena_tables(L):
    t = np.arange(L, dtype=np.float32) / np.float32(L)

    def emb(tt):
        ang = (np.float32(2.0 * math.pi) * tt[:, None] * np.arange(1, HY_BANDS + 1, dtype=np.float32)[None, :]).astype(np.float32)
        e = np.concatenate([tt[:, None], np.cos(ang), np.sin(ang)], axis=-1).astype(np.float32)
        return np.pad(e, ((0, 0), (0, 128 - HY_EMB)))

    tb = np.concatenate([t[:1], t[:0:-1]])
    kk = np.arange(L, dtype=np.int64)
    ang = (2.0 * math.pi / (2 * L)) * ((kk[:, None] * kk[None, :]) % (2 * L)).astype(np.float64)
    return dict(emb_f=jnp.asarray(emb(t), BF16), emb_b=jnp.asarray(emb(tb), BF16),
                t_f=jnp.asarray(t[:, None]), t_b=jnp.asarray(tb[:, None]),
                cz=jnp.asarray(np.cos(ang), BF16), sz=jnp.asarray(np.sin(ang), BF16))


def _rope_tables(L):
    nf = ML_DH // 4
    freqs = (np.float32(ROPE_BASE) ** (-np.arange(nf, dtype=np.float32) / np.float32(nf))).astype(np.float32)
    pos = np.arange(L)
    out_c, out_s = [], []
    for p in (pos // GRID_W, pos % GRID_W):
        ang = p.astype(np.float32)[:, None] * freqs[None, :]
        c, s = np.cos(ang).astype(np.float32), np.sin(ang).astype(np.float32)
        out_c += [c, c]
        out_s += [-s, s]
    return jnp.asarray(np.concatenate(out_c, axis=-1)), jnp.asarray(np.concatenate(out_s, axis=-1))


def kernel(x_prompt, x_sample, state_rglru, state_ssd, state_mlstm_C, state_mlstm_n, state_mlstm_m, c, c_ctx,
           ada_w, ada_b, ln_g, ln_b, ffn_wg, ffn_wu, ffn_wd, w_in, rg_conv_w, rg_conv_b, rg_gate_w, rg_gate_b,
           rg_lambda, hy_conv_w, hy_conv_b, hy_w1, hy_b1, hy_w2, hy_b2, hy_w3, hy_decay, hy_bias,
           ssd_conv_w, ssd_conv_b, ssd_dt_bias, ssd_A_log, ssd_D, ssd_norm_w, ml_gate_b, ml_norm_w,
           branch_w, mix_out):
    bp, lp_len = x_prompt.shape[:2]
    bs, ls = x_sample.shape[:2]
    mp, ms = bp * lp_len, bs * ls
    ncp, ncs = lp_len // CHUNK, ls // CHUNK

    def midx(i, bm):
        return jnp.where(i < mp // bm, 0, 1 + (i - mp // bm) // (ls // bm))

    cond = jnp.concatenate([c_ctx[None], c, jnp.zeros((8 - 1 - bs, D_MODEL), F32)], axis=0)
    s = jax.nn.silu(cond).astype(BF16)
    mod = _ada_call(s, ada_w, ada_b.reshape(DEPTH, 1, N_MOD * D_MODEL))
    mod = mod[:, :1 + bs].reshape(DEPTH, 1 + bs, N_MOD, 1, D_MODEL)

    def mod_row(l, j):
        return mod[l, :, j]

    o = IN_OFFSETS

    def wseg(a, b):
        return w_in[:, :, a:b].astype(BF16)

    w_small = jnp.concatenate(
        [wseg(o[1], o[2]), wseg(0, o[1]), wseg(o[2], o[4]),
         wseg(o[4], o[5]), wseg(o[6], o[7]),
         jnp.zeros((DEPTH, D_MODEL, O_QKVO - O_SMALL - 2 * SSD_H - 4 * ML_NH), BF16),
         wseg(o[5], o[6])], axis=-1)
    w_merge = wseg(o[7], w_in.shape[-1])
    wg_b, wu_b, wd_b = ffn_wg.astype(BF16), ffn_wu.astype(BF16), ffn_wd.astype(BF16)
    bw_b, mo_b = branch_w.astype(BF16), mix_out.astype(BF16)
    gw_b = rg_gate_w.astype(BF16)
    hw1 = jnp.pad(hy_w1, ((0, 0), (0, 128 - HY_EMB), (0, 0))).astype(BF16)
    hw2, hw3 = hy_w2.astype(BF16), hy_w3.astype(BF16)

    tabs_p, tabs_s = _hyena_tables(lp_len), _hyena_tables(ls)
    rope_s = _rope_tables(ls)

    x = jnp.concatenate([x_prompt.reshape(mp, D_MODEL), x_sample.reshape(ms, D_MODEL)], axis=0)
    h = _modulate_call(x, mod_row(0, 0), mod_row(0, 1), midx)

    new_states = []
    for l in range(DEPTH):
        lng = ln_g[l].reshape(3, 1, D_MODEL)
        lnb = ln_b[l].reshape(3, 1, D_MODEL)

        a = _ffn_up_call(h, wg_b[l, 0], wu_b[l, 0])
        x, h = _res_ln_call(a, wd_b[l, 0], x, mod_row(l, 2), lng[0], lnb[0], mod_row(l, 3), mod_row(l, 4),
                            midx, 0.5, True)

        proj = _in_proj_call(h, w_small[l])
        hu, xa, xbc = _conv_call(proj, hy_conv_w[l], hy_conv_b[l][None], rg_conv_w[l], rg_conv_b[l][None],
                                 ssd_conv_w[l], ssd_conv_b[l][None], mp // CHUNK, ncp, ncs)

        gb = rg_gate_b[l].reshape(4, W_RG)
        a_p, rg_last = _rglru_call(xa, proj, gw_b[l], gb, rg_lambda[l], None, 0, bp, ncp)
        a_s, _ = _rglru_call(xa, proj, gw_b[l], gb, rg_lambda[l], state_rglru[:, l], mp // CHUNK, bs, ncs)

        outs_b = []
        for tabs, L, kb, tc, base, n_seq in ((tabs_p, lp_len, lp_len, W_HY, 0, bp), (tabs_s, ls, 256, 256, mp // ls, bs)):
            fa, fb, nyq = _hy_filt_call(L, tabs['emb_f'], tabs['emb_b'], tabs['t_f'], tabs['t_b'], hw1[l], hy_b1[l][None],
                                        hw2[l], hy_b2[l][None], hw3[l], hy_decay[l][None])
            fre, fim = _hy_spec_call(L, kb, fa, fb, tabs['cz'], tabs['sz'])
            outs_b.append(_hy_conv_call(hu, fre, fim, nyq, hy_bias[l].reshape(1, HY_ORDER * W_HY), tabs['cz'], tabs['sz'],
                                        L, kb, tc, base, n_seq))

        dvec = jnp.repeat(ssd_D[l], SSD_P)[None]
        nw = ssd_norm_w[l][None]
        c_p, ssd_last = _ssd_call(xbc, proj, ssd_dt_bias[l], ssd_A_log[l], dvec, nw, None, 0, bp, ncp)
        c_s, _ = _ssd_call(xbc, proj, ssd_dt_bias[l], ssd_A_log[l], dvec, nw, state_ssd[:, l], mp // CHUNK, bs, ncs)

        mnw = ml_norm_w[l][None]
        d_p, cl, nl, ml = _mlstm_call(proj, ml_gate_b[l], mnw, None, None, 0, bp, ncp)
        d_s, _, _, _ = _mlstm_call(proj, ml_gate_b[l], mnw, rope_s,
                                   (state_mlstm_C[:, l], state_mlstm_n[:, l], state_mlstm_m[:, l]), mp // CHUNK, bs, ncs)
        new_states.append((rg_last, ssd_last, cl, nl, ml))

        branches = [jnp.concatenate(pair, axis=0) for pair in ((a_p, a_s), tuple(outs_b), (c_p, c_s), (d_p, d_s))]
        merged = _merge_call(h, branches, w_merge[l], bw_b[l])
        x, h = _res_ln_call(merged, mo_b[l], x, mod_row(l, 5), lng[1], lnb[1], mod_row(l, 6), mod_row(l, 7),
                            midx, 1.0, True)

        a = _ffn_up_call(h, wg_b[l, 1], wu_b[l, 1])
        last = l == DEPTH - 1
        nl_ = l if last else l + 1
        x, h = _res_ln_call(a, wd_b[l, 1], x, mod_row(l, 8), lng[2], lnb[2], mod_row(nl_, 0), mod_row(nl_, 1),
                            midx, 0.5, not last)

    yp = x[:mp].reshape(bp, lp_len, D_MODEL)
    ys = x[mp:].reshape(bs, ls, D_MODEL)
    outs = tuple(jnp.stack([new_states[l][j] for l in range(DEPTH)], axis=1) for j in range(5))
    return (yp, ys) + outs
```

```python
import functools
import math

import jax
import jax.numpy as jnp
import numpy as np
from jax import lax
from jax.experimental import pallas as pl
from jax.experimental.pallas import tpu as pltpu

D_MODEL = 4096
DEPTH = 2
GRID_W = 64
N_BRANCH = 4
W_BRANCH = D_MODEL // N_BRANCH
D_FF = 2 * D_MODEL
N_MOD = 9
CHUNK = 128
W_RG = W_BRANCH
RG_BLOCKS = 8
RG_BW = W_RG // RG_BLOCKS
RG_C = 8.0
W_HY = W_BRANCH
HY_ORDER = 2
HY_BANDS = 16
HY_EMB = 2 * HY_BANDS + 1
HY_FH = 64
SSD_DI = W_BRANCH
SSD_P = 64
SSD_H = SSD_DI // SSD_P
SSD_N = 128
SSD_G = 2
SSD_CONV_CH = SSD_DI + 2 * SSD_G * SSD_N
W_ML = W_BRANCH
ML_NH = 4
ML_DH = W_ML // ML_NH
ROPE_BASE = 10000.0
ALPHA = (2 * DEPTH) ** 0.25

IN_SIZES = (W_RG, W_RG, 3 * W_HY, SSD_DI, SSD_CONV_CH, 2 * SSD_H, 4 * W_ML, 4 * ML_NH, N_BRANCH * D_MODEL)
IN_OFFSETS = tuple(int(s) for s in np.cumsum(IN_SIZES)[:-1])

F32 = jnp.float32
BF16 = jnp.bfloat16
HI = lax.Precision.HIGHEST
NT = (((1,), (1,)), ((), ()))
TN = (((0,), (0,)), ((), ()))

O_HY = 0
O_RGX = O_HY + 3 * W_HY
O_RGY = O_RGX + W_RG
O_SSDZ = O_RGY + W_RG
O_XBC = O_SSDZ + SSD_DI
O_SMALL = O_XBC + SSD_CONV_CH
W_SMALL = 128
O_QKVO = 8192
N_PROJ = O_QKVO + 4 * W_ML

VMEM_LIMIT = 56 * 1024 * 1024


def _cparams(sem):
    return pltpu.CompilerParams(dimension_semantics=sem, vmem_limit_bytes=VMEM_LIMIT)


def _dot(a, b):
    return jnp.dot(a, b, preferred_element_type=F32)


def _dg(a, b, dims, precision=None):
    return lax.dot_general(a, b, dims, preferred_element_type=F32, precision=precision)


def _ada_body(s_ref, w_ref, b_ref, o_ref):
    o_ref[...] = _dot(s_ref[...], w_ref[...].astype(BF16)) + b_ref[...]


def _ada_call(s, ada_w, ada_b):
    bn = 512
    n = ada_w.shape[-1]
    return pl.pallas_call(
        _ada_body,
        grid=(DEPTH, n // bn),
        in_specs=[pl.BlockSpec((8, D_MODEL), lambda l, j: (0, 0)),
                  pl.BlockSpec((None, D_MODEL, bn), lambda l, j: (l, 0, j)),
                  pl.BlockSpec((None, 1, bn), lambda l, j: (l, 0, j))],
        out_specs=pl.BlockSpec((None, 8, bn), lambda l, j: (l, 0, j)),
        out_shape=jax.ShapeDtypeStruct((DEPTH, 8, n), F32),
        compiler_params=_cparams(("parallel", "parallel")),
        name="ada_mod",
    )(s, ada_w, ada_b)


def _modulate_body(x_ref, sh_ref, sc_ref, o_ref):
    o_ref[...] = (x_ref[...] * (1.0 + sc_ref[...]) + sh_ref[...]).astype(BF16)


def _modulate_call(x, shift, scale, midx):
    m = x.shape[0]
    bm = 256
    mspec = pl.BlockSpec((None, 1, D_MODEL), lambda i: (midx(i, bm), 0, 0))
    return pl.pallas_call(
        _modulate_body,
        grid=(m // bm,),
        in_specs=[pl.BlockSpec((bm, D_MODEL), lambda i: (i, 0)), mspec, mspec],
        out_specs=pl.BlockSpec((bm, D_MODEL), lambda i: (i, 0)),
        out_shape=jax.ShapeDtypeStruct((m, D_MODEL), BF16),
        compiler_params=_cparams(("parallel",)),
        name="modulate",
    )(x, shift, scale)


def _ffn_up_body(h_ref, wg_ref, wu_ref, o_ref):
    h = h_ref[...]
    g = _dot(h, wg_ref[...])
    u = _dot(h, wu_ref[...])
    o_ref[...] = (g * jax.nn.sigmoid(g) * u).astype(BF16)


def _wspec(lead, block, index_fn):
    lead = tuple(lead)
    return pl.BlockSpec((None,) * len(lead) + tuple(block), lambda *g: lead + tuple(index_fn(*g)))


def _ffn_up_call(h, wg, wu, lead):
    m = h.shape[0]
    bm, bn = 1024, 512
    return pl.pallas_call(
        _ffn_up_body,
        grid=(m // bm, D_FF // bn),
        in_specs=[pl.BlockSpec((bm, D_MODEL), lambda i, j: (i, 0)),
                  _wspec(lead, (D_MODEL, bn), lambda i, j: (0, j)),
                  _wspec(lead, (D_MODEL, bn), lambda i, j: (0, j))],
        out_specs=pl.BlockSpec((bm, bn), lambda i, j: (i, j)),
        out_shape=jax.ShapeDtypeStruct((m, D_FF), BF16),
        compiler_params=_cparams(("parallel", "parallel")),
        name="ffn_up",
    )(h, wg, wu)


def _in_proj_body(h_ref, w_ref, o_ref):
    o_ref[...] = _dot(h_ref[...], w_ref[...])


def _in_proj_call(h, w, lead):
    m = h.shape[0]
    bm, bn = 1024, 512
    return pl.pallas_call(
        _in_proj_body,
        grid=(m // bm, N_PROJ // bn),
        in_specs=[pl.BlockSpec((bm, D_MODEL), lambda i, j: (i, 0)),
                  _wspec(lead, (D_MODEL, bn), lambda i, j: (0, j))],
        out_specs=pl.BlockSpec((bm, bn), lambda i, j: (i, j)),
        out_shape=jax.ShapeDtypeStruct((m, N_PROJ), F32),
        compiler_params=_cparams(("parallel", "parallel")),
        name="in_proj",
    )(h, w)


def _merge_body(h_ref, *refs, p_blocks):
    bp, bs, wg, wb, o_ref = refs[0:4], refs[4:8], refs[8:12], refs[12:16], refs[16]
    h = h_ref[...]
    is_prompt = pl.program_id(0) < p_blocks
    acc = None
    for j in range(N_BRANCH):
        br = jnp.where(is_prompt, bp[j][...], bs[j][...])
        t = jax.nn.sigmoid(_dot(h, wg[j][...])) * _dot(br, wb[j][...])
        acc = t if acc is None else acc + t
    o_ref[...] = acc.astype(BF16)


def _merge_call(h, br_prompt, br_sample, w_merge, branch_w, lead):
    m = h.shape[0]
    bm, bn = 512, 256
    nb = D_MODEL // bn
    p_blocks = br_prompt[0].shape[0] // bm
    bp_specs = [pl.BlockSpec((bm, W_BRANCH), lambda i, n: (jnp.minimum(i, p_blocks - 1), 0)) for _ in range(N_BRANCH)]
    bs_specs = [pl.BlockSpec((bm, W_BRANCH), lambda i, n: (jnp.maximum(i - p_blocks, 0), 0)) for _ in range(N_BRANCH)]
    wg_specs = [_wspec(lead, (D_MODEL, bn), functools.partial(lambda i, n, j: (0, j * nb + n), j=j))
                for j in range(N_BRANCH)]
    wb_specs = [_wspec(tuple(lead) + (j,), (W_BRANCH, bn), lambda i, n: (0, n)) for j in range(N_BRANCH)]
    return pl.pallas_call(
        functools.partial(_merge_body, p_blocks=p_blocks),
        grid=(m // bm, nb),
        in_specs=[pl.BlockSpec((bm, D_MODEL), lambda i, n: (i, 0))] + bp_specs + bs_specs + wg_specs + wb_specs,
        out_specs=pl.BlockSpec((bm, bn), lambda i, n: (i, n)),
        out_shape=jax.ShapeDtypeStruct((m, D_MODEL), BF16),
        compiler_params=_cparams(("parallel", "parallel")),
        name="merge",
    )(h, *br_prompt, *br_sample, *([w_merge] * N_BRANCH), *([branch_w] * N_BRANCH))


def _res_ln_body(a_ref, w_ref, x_ref, gate_ref, lng_ref, lnb_ref, nsh_ref, nsc_ref, y_ref, hn_ref, st_ref, *,
                 coef, nk, bm, emit_next):
    k = pl.program_id(1)

    @pl.when(k == 0)
    def _():
        y_ref[...] = _dot(a_ref[...], w_ref[...])

    @pl.when(k > 0)
    def _():
        y_ref[...] += _dot(a_ref[...], w_ref[...])

    @pl.when(k == nk - 1)
    def _():
        rows = 8
        gate = coef * gate_ref[...]
        lng, lnb = lng_ref[...], lnb_ref[...]
        nsc, nsh = 1.0 + nsc_ref[...], nsh_ref[...]

        def rows_of(c):
            return pl.ds(pl.multiple_of(c * rows, rows), rows)

        def sweep_mean(c, carry):
            rs = rows_of(c)
            r = ALPHA * x_ref[rs, :] + gate * y_ref[rs, :]
            y_ref[rs, :] = r
            st_ref[rs, 0:1] = jnp.mean(r, axis=-1, keepdims=True)
            return carry

        def sweep_var(c, carry):
            rs = rows_of(c)
            d = y_ref[rs, :] - st_ref[rs, 0:1]
            st_ref[rs, 1:2] = lax.rsqrt(jnp.mean(d * d, axis=-1, keepdims=True) + 1e-5)
            return carry

        def sweep_out(c, carry):
            rs = rows_of(c)
            o = (y_ref[rs, :] - st_ref[rs, 0:1]) * st_ref[rs, 1:2] * lng + lnb
            y_ref[rs, :] = o
            if emit_next:
                hn_ref[rs, :] = (o * nsc + nsh).astype(BF16)
            return carry

        for sweep, unroll in ((sweep_mean, 8), (sweep_var, 8), (sweep_out, 4)):
            lax.fori_loop(0, bm // rows, sweep, 0, unroll=unroll)
        if not emit_next:
            hn_ref[...] = jnp.zeros_like(hn_ref)


def _res_ln_call(a, w, lead, x, gate, lng, lnb, nshift, nscale, midx, coef, emit_next):
    m, kdim = a.shape
    bm, bk = 512, 512
    nk = kdim // bk
    hn_rows = bm if emit_next else 8
    mspec = pl.BlockSpec((None, 1, D_MODEL), lambda i, k: (midx(i, bm), 0, 0))
    vspec = pl.BlockSpec((1, D_MODEL), lambda i, k: (0, 0))
    y, hn = pl.pallas_call(
        functools.partial(_res_ln_body, coef=coef, nk=nk, bm=bm, emit_next=emit_next),
        grid=(m // bm, nk),
        in_specs=[pl.BlockSpec((bm, bk), lambda i, k: (i, k)),
                  _wspec(lead, (bk, D_MODEL), lambda i, k: (k, 0)),
                  pl.BlockSpec((bm, D_MODEL), lambda i, k: (i, 0)),
                  mspec, vspec, vspec, mspec, mspec],
        out_specs=[pl.BlockSpec((bm, D_MODEL), lambda i, k: (i, 0)),
                   pl.BlockSpec((hn_rows, D_MODEL), lambda i, k: (i, 0))],
        out_shape=[jax.ShapeDtypeStruct((m, D_MODEL), F32),
                   jax.ShapeDtypeStruct((m if emit_next else 8 * (m // bm), D_MODEL), BF16)],
        scratch_shapes=[pltpu.VMEM((bm, 128), F32)],
        compiler_params=_cparams(("parallel", "arbitrary")),
        name="res_ln",
    )(a, w, x, gate, lng, lnb, nshift, nscale)
    return y, hn


def _conv_piece(pad_ref, w_ref, b_ref, o_ref, c0, width, taps, pad_l, act):
    cs = slice(c0, c0 + width)
    acc = b_ref[:, cs]
    for j in range(taps):
        r0 = 8 - pad_l + j
        acc = acc + w_ref[j:j + 1, cs] * pad_ref[r0:r0 + CHUNK, cs]
    o_ref[:, cs] = act(acc)


def _conv_body(hy_p, hy_c, hy_n, rg_p, rg_c, rg_n, xb_p, xb_c, xb_n,
               hw_ref, hb_ref, rw_ref, rb_ref, xw_ref, xb_ref,
               hu_ref, xa_ref, xbc_ref, pad_ref, *, chunk_pos):
    first, last = chunk_pos(pl.program_id(0))
    for prev, cur, nxt, w_ref, b_ref, o_ref, width, taps, pad_l, act in (
            (hy_p, hy_c, hy_n, hw_ref, hb_ref, hu_ref, 3 * W_HY, 3, 1, lambda v: v),
            (rg_p, rg_c, rg_n, rw_ref, rb_ref, xa_ref, W_RG, 4, 2, lambda v: v),
            (xb_p, xb_c, xb_n, xw_ref, xb_ref, xbc_ref, SSD_CONV_CH, 4, 2, jax.nn.silu)):
        pad_ref[0:8, 0:width] = jnp.where(first, 0.0, prev[...])
        pad_ref[8:8 + CHUNK, 0:width] = cur[...]
        pad_ref[8 + CHUNK:16 + CHUNK, 0:width] = jnp.where(last, 0.0, nxt[...])
        for c0 in range(0, width, 512):
            _conv_piece(pad_ref, w_ref, b_ref, o_ref, c0, min(512, width - c0), taps, pad_l, act)


def _conv_call(proj, hy_w, hy_b, rg_w, rg_b, xb_w, xb_b, mp_chunks, ncp, ncs):
    m = proj.shape[0]
    n8 = m // 8

    def chunk_pos(i):
        pos = jnp.where(i < mp_chunks, i % ncp, (i - mp_chunks) % ncs)
        n = jnp.where(i < mp_chunks, ncp, ncs)
        return pos == 0, pos == n - 1

    def trio(width, col):
        return [pl.BlockSpec((8, width), lambda i: (jnp.maximum(i * (CHUNK // 8) - 1, 0), col)),
                pl.BlockSpec((CHUNK, width), lambda i: (i, col)),
                pl.BlockSpec((8, width), lambda i: (jnp.minimum((i + 1) * (CHUNK // 8), n8 - 1), col))]

    def full(a):
        return pl.BlockSpec(a.shape, lambda i: (0, 0))

    params = (hy_w, hy_b, rg_w, rg_b, xb_w, xb_b)
    return pl.pallas_call(
        functools.partial(_conv_body, chunk_pos=chunk_pos),
        grid=(m // CHUNK,),
        in_specs=trio(3 * W_HY, O_HY // (3 * W_HY)) + trio(W_RG, O_RGX // W_RG)
        + trio(SSD_CONV_CH, O_XBC // SSD_CONV_CH) + [full(a) for a in params],
        out_specs=[pl.BlockSpec((CHUNK, 3 * W_HY), lambda i: (i, 0)),
                   pl.BlockSpec((CHUNK, W_RG), lambda i: (i, 0)),
                   pl.BlockSpec((CHUNK, SSD_CONV_CH), lambda i: (i, 0))],
        out_shape=[jax.ShapeDtypeStruct((m, 3 * W_HY), F32),
                   jax.ShapeDtypeStruct((m, W_RG), F32),
                   jax.ShapeDtypeStruct((m, SSD_CONV_CH), F32)],
        scratch_shapes=[pltpu.VMEM((CHUNK + 16, 3 * W_HY), F32)],
        compiler_params=_cparams(("parallel",)),
        name="dw_conv",
    )(*([proj] * 9), *params)


def _chunk_of(j, nc):
    return jnp.where(j < nc, j, 2 * nc - 1 - j)


def _row_map(base, nc, col):
    return lambda b, j: (base + b * nc + _chunk_of(j, nc), col)


def _out_map(nc):
    return lambda b, j: (b * nc + jnp.where(j < nc, nc - 1, 2 * nc - 1 - j), 0)


def _tri(rev):
    r = lax.broadcasted_iota(jnp.int32, (CHUNK, CHUNK), 0)
    c = lax.broadcasted_iota(jnp.int32, (CHUNK, CHUNK), 1)
    return (c >= r) if rev else (c <= r)


def _two_phase(nc, run):
    j = pl.program_id(1)

    @pl.when(j < nc)
    def _():
        run(0, j == 0, j == nc - 1, j)

    @pl.when(j >= nc)
    def _():
        run(1, j == nc, j == 2 * nc - 1, 2 * nc - 1 - j)


def _rglru_body(xa_ref, y_ref, gw_ref, gb_ref, lam_ref, h0_ref, o_ref, hl_ref,
                yf_ref, a_ref, u_ref, hb_ref, hc_ref, *, nc, init):
    def run(d, is_first, is_last, ci):
        x = xa_ref[...]
        xb = x.astype(BF16)
        sp = jax.nn.softplus(-lam_ref[d:d + 1, :])
        for n in range(RG_BLOCKS):
            cs = slice(n * RG_BW, (n + 1) * RG_BW)
            r = jax.nn.sigmoid(_dot(xb[:, cs], gw_ref[d, 0, n]) + gb_ref[2 * d:2 * d + 1, cs])
            i = jax.nn.sigmoid(_dot(xb[:, cs], gw_ref[d, 1, n]) + gb_ref[2 * d + 1:2 * d + 2, cs])
            log_a = -RG_C * r * sp[:, cs]
            th = jnp.tanh(log_a)
            a_ref[:, cs] = jnp.exp(log_a)
            u_ref[:, cs] = jnp.sqrt(-2.0 * th / (1.0 - th)) * (i * x[:, cs])

        @pl.when(is_first)
        def _():
            hc_ref[0:1, :] = h0_ref[d:d + 1, :] if init else jnp.zeros((1, W_RG), F32)

        def step(s, h):
            t = (CHUNK - 1 - s) if d else s
            h = a_ref[pl.ds(t, 1), :] * h + u_ref[pl.ds(t, 1), :]
            hb_ref[pl.ds(t, 1), :] = h
            return h

        h = lax.fori_loop(0, CHUNK, step, hc_ref[0:1, :], unroll=8)
        hc_ref[0:1, :] = h

        @pl.when(is_last)
        def _():
            hl_ref[d:d + 1, :] = h

        r0 = pl.multiple_of(ci * CHUNK, CHUNK)
        if d == 0:
            yf_ref[pl.ds(r0, CHUNK), :] = hb_ref[...]
        else:
            o_ref[...] = ((yf_ref[pl.ds(r0, CHUNK), :] + hb_ref[...]) * jax.nn.gelu(y_ref[...])).astype(BF16)

    _two_phase(nc, run)


def _rglru_call(xa, proj, gw, gb, lam, h0, base, n_seq, nc):
    init = h0 is not None
    if not init:
        h0 = jnp.zeros((n_seq, 2, W_RG), F32)
    L = nc * CHUNK
    st_spec = pl.BlockSpec((None, 2, W_RG), lambda b, j: (b, 0, 0))
    return pl.pallas_call(
        functools.partial(_rglru_body, nc=nc, init=init),
        grid=(n_seq, 2 * nc),
        in_specs=[pl.BlockSpec((CHUNK, W_RG), _row_map(base, nc, 0)),
                  pl.BlockSpec((CHUNK, W_RG), _row_map(base, nc, O_RGY // W_RG)),
                  pl.BlockSpec(gw.shape, lambda b, j: (0, 0, 0, 0, 0)),
                  pl.BlockSpec(gb.shape, lambda b, j: (0, 0)),
                  pl.BlockSpec(lam.shape, lambda b, j: (0, 0)),
                  st_spec],
        out_specs=[pl.BlockSpec((CHUNK, W_RG), _out_map(nc)), st_spec],
        out_shape=[jax.ShapeDtypeStruct((n_seq * L, W_RG), BF16),
                   jax.ShapeDtypeStruct((n_seq, 2, W_RG), F32)],
        scratch_shapes=[pltpu.VMEM((L, W_RG), F32), pltpu.VMEM((CHUNK, W_RG), F32), pltpu.VMEM((CHUNK, W_RG), F32),
                        pltpu.VMEM((CHUNK, W_RG), F32), pltpu.VMEM((8, W_RG), F32)],
        compiler_params=_cparams(("parallel", "arbitrary")),
        name="rglru",
    )(xa, proj, gw, gb, lam, h0)


def _ssd_body(xbc_ref, z_ref, sm_ref, dtb_r, dtb_c, al_r, al_c, dvec_ref, nw_ref, s0_ref, o_ref, sl_ref,
              yf_ref, yb_ref, s_ref, *, nc, init):
    def run(d, is_first, is_last, ci):
        hs = slice(SSD_H * d, SSD_H * (d + 1))
        sm = sm_ref[...]
        smt = sm.T
        dt_c = jax.nn.softplus(sm[:, hs] + dtb_r[:, hs])
        dt_r = jax.nn.softplus(smt[hs, :] + dtb_c[hs, :])
        mask = _tri(d)
        tri = jnp.where(mask, 1.0, 0.0).astype(F32)
        cs_c = jnp.dot(tri, dt_c * -jnp.exp(al_r[:, hs]), preferred_element_type=F32, precision=HI)
        cs_r = _dg(dt_r * -jnp.exp(al_c[hs, :]), tri, NT, HI)
        e = 0 if d else CHUNK - 1
        tot = cs_c[e:e + 1, :]
        w_end = jnp.exp(tot - cs_c) * dt_c
        ecs = jnp.exp(cs_c)
        etot = jnp.exp(tot)

        @pl.when(is_first)
        def _():
            s_ref[...] = s0_ref[d] if init else jnp.zeros(s_ref.shape, F32)

        xs = xbc_ref[:, 0:SSD_DI]
        xsb = xs.astype(BF16)
        for g in range(SSD_G):
            bb = xbc_ref[:, SSD_DI + g * SSD_N:SSD_DI + (g + 1) * SSD_N].astype(BF16)
            cb = xbc_ref[:, SSD_DI + (SSD_G + g) * SSD_N:SSD_DI + (SSD_G + g + 1) * SSD_N].astype(BF16)
            gmat = _dg(cb, bb, NT)
            for hh in range(SSD_H // SSD_G):
                h = g * (SSD_H // SSD_G) + hh
                ps = slice(h * SSD_P, (h + 1) * SSD_P)
                seg = cs_c[:, h:h + 1] - cs_r[h:h + 1, :]
                dec = jnp.exp(jnp.where(mask, seg, -jnp.inf))
                mh = (gmat * dec * dt_r[h:h + 1, :]).astype(BF16)
                sh = s_ref[h]
                yb_ref[:, ps] = _dot(mh, xsb[:, ps]) + _dg(cb, sh.astype(BF16), NT) * ecs[:, h:h + 1]
                xw = (xs[:, ps] * w_end[:, h:h + 1]).astype(BF16)
                s_ref[h] = etot[:, h:h + 1] * sh + _dg(xw, bb, TN)

        @pl.when(is_last)
        def _():
            sl_ref[d] = s_ref[...]

        r0 = pl.multiple_of(ci * CHUNK, CHUNK)
        if d == 0:
            yf_ref[pl.ds(r0, CHUNK), :] = yb_ref[...]
        else:
            yc = yf_ref[pl.ds(r0, CHUNK), :] + yb_ref[...] + dvec_ref[...] * xs
            v = yc * jax.nn.silu(z_ref[...])
            o_ref[...] = (v * lax.rsqrt(jnp.mean(v * v, axis=-1, keepdims=True) + 1e-6) * nw_ref[...]).astype(BF16)

    _two_phase(nc, run)


def _ssd_call(xbc, proj, dt_bias, a_log, dvec, norm_w, s0, base, n_seq, nc):
    init = s0 is not None
    if not init:
        s0 = jnp.zeros((n_seq, 2, 8, 8, SSD_N), F32)
    L = nc * CHUNK
    st_shape = (SSD_H, SSD_P, SSD_N)
    dtb_r, al_r = dt_bias.reshape(1, 2 * SSD_H), a_log.reshape(1, 2 * SSD_H)
    dtb_c, al_c = dt_bias.reshape(2 * SSD_H, 1), a_log.reshape(2 * SSD_H, 1)

    def full(a):
        return pl.BlockSpec(a.shape, lambda b, j: (0,) * a.ndim)

    s0_spec = pl.BlockSpec((None,) + s0.shape[1:], lambda b, j: (b, 0, 0, 0, 0))
    return pl.pallas_call(
        functools.partial(_ssd_body, nc=nc, init=init),
        grid=(n_seq, 2 * nc),
        in_specs=[pl.BlockSpec((CHUNK, SSD_CONV_CH), _row_map(base, nc, 0)),
                  pl.BlockSpec((CHUNK, SSD_DI), _row_map(base, nc, O_SSDZ // SSD_DI)),
                  pl.BlockSpec((CHUNK, W_SMALL), _row_map(base, nc, O_SMALL // W_SMALL)),
                  full(dtb_r), full(dtb_c), full(al_r), full(al_c), full(dvec), full(norm_w), s0_spec],
        out_specs=[pl.BlockSpec((CHUNK, SSD_DI), _out_map(nc)),
                   pl.BlockSpec((None, 2) + st_shape, lambda b, j: (b, 0, 0, 0, 0))],
        out_shape=[jax.ShapeDtypeStruct((n_seq * L, SSD_DI), BF16),
                   jax.ShapeDtypeStruct((n_seq, 2) + st_shape, F32)],
        scratch_shapes=[pltpu.VMEM((L, SSD_DI), F32), pltpu.VMEM((CHUNK, SSD_DI), F32), pltpu.VMEM(st_shape, F32)],
        compiler_params=_cparams(("parallel", "arbitrary")),
        name="ssd",
    )(xbc, proj, proj, dtb_r, dtb_c, al_r, al_c, dvec, norm_w, s0)


def _mlstm_body(qkvo_ref, sm_ref, gb_r, gb_c, nw_ref, cos_ref, sin_ref, c0_ref, n0_ref, m0_ref,
                o_ref, cl_ref, nl_ref, ml_ref, hf_ref, hb_ref, c_ref, n_ref, m_ref, *, nc, init, rope):
    def run(d, is_first, is_last, ci):
        sm = sm_ref[...]
        smt = sm.T
        o_i, o_f = 2 * SSD_H + 8 * d, 2 * SSD_H + 8 * d + ML_NH
        i_c = sm[:, o_i:o_i + ML_NH] + gb_r[:, 8 * d:8 * d + ML_NH]
        i_r = smt[o_i:o_i + ML_NH, :] + gb_c[8 * d:8 * d + ML_NH, :]
        lf_c = jax.nn.log_sigmoid(sm[:, o_f:o_f + ML_NH] + gb_r[:, 8 * d + ML_NH:8 * d + 2 * ML_NH])
        lf_r = jax.nn.log_sigmoid(smt[o_f:o_f + ML_NH, :] + gb_c[8 * d + ML_NH:8 * d + 2 * ML_NH, :])
        mask = _tri(d)
        tri = jnp.where(mask, 1.0, 0.0).astype(F32)
        b_c = jnp.dot(tri, lf_c, preferred_element_type=F32, precision=HI)
        b_r = _dg(lf_r, tri, NT, HI)
        e = 0 if d else CHUNK - 1
        b_end = b_c[e:e + 1, :]

        @pl.when(is_first)
        def _():
            if init:
                c_ref[...] = c0_ref[d]
                n_ref[0:ML_NH, :] = n0_ref[d]
                m_ref[0:1, 0:ML_NH] = m0_ref[d:d + 1, :]
            else:
                c_ref[...] = jnp.zeros(c_ref.shape, F32)
                n_ref[...] = jnp.zeros(n_ref.shape, F32)
                m_ref[...] = jnp.zeros(m_ref.shape, F32)

        def rot(t):
            if not rope:
                return t
            parts = []
            for p in range(ML_DH // 128):
                tp = t[:, p * 128:(p + 1) * 128]
                parts.append(tp * cos_ref[:, p * 128:(p + 1) * 128]
                             + pltpu.roll(tp, 64, axis=1) * sin_ref[:, p * 128:(p + 1) * 128])
            return jnp.concatenate(parts, axis=-1)

        for h in range(ML_NH):
            hsl = slice(h * ML_DH, (h + 1) * ML_DH)
            q = rot(qkvo_ref[:, h * ML_DH:(h + 1) * ML_DH])
            k = rot(qkvo_ref[:, W_ML + h * ML_DH:W_ML + (h + 1) * ML_DH] * (ML_DH ** -0.5))
            v = qkvo_ref[:, 2 * W_ML + h * ML_DH:2 * W_ML + (h + 1) * ML_DH]
            qb, kb = q.astype(BF16), k.astype(BF16)
            m_prev = m_ref[0:1, h:h + 1]
            dmat = jnp.where(mask, b_c[:, h:h + 1] - b_r[h:h + 1, :] + i_r[h:h + 1, :], -jnp.inf)
            g0 = b_c[:, h:h + 1] + m_prev
            mt = jnp.maximum(g0, jnp.max(dmat, axis=1, keepdims=True))
            w = jnp.exp(dmat - mt)
            w0 = jnp.exp(g0 - mt)
            s = _dg(qb, kb, NT) * w
            ch = c_ref[h]
            nh = n_ref[h:h + 1, :]
            num = _dot(s.astype(BF16), v.astype(BF16)) + w0 * _dg(qb, ch.astype(BF16), NT)
            den = jnp.sum(s, axis=1, keepdims=True) + w0 * jnp.sum(q * nh, axis=1, keepdims=True)
            hb_ref[:, hsl] = num / jnp.maximum(jnp.abs(den), jnp.exp(-mt))
            d_end = b_end[:, h:h + 1] - b_c[:, h:h + 1] + i_c[:, h:h + 1]
            g0e = b_end[:, h:h + 1] + m_prev
            m_new = jnp.maximum(g0e, jnp.max(d_end, axis=0, keepdims=True))
            we = jnp.exp(d_end - m_new)
            w0e = jnp.exp(g0e - m_new)
            c_ref[h] = w0e * ch + _dg((v * we).astype(BF16), kb, TN)
            n_ref[h:h + 1, :] = w0e * nh + jnp.sum(we * k, axis=0, keepdims=True)
            m_ref[0:1, h:h + 1] = m_new

        @pl.when(is_last)
        def _():
            cl_ref[d] = c_ref[...]
            nl_ref[d] = n_ref[0:ML_NH, :]
            ml_ref[d:d + 1, :] = m_ref[0:1, 0:ML_NH]

        r0 = pl.multiple_of(ci * CHUNK, CHUNK)
        if d == 0:
            hf_ref[pl.ds(r0, CHUNK), :] = hb_ref[...]
        else:
            for h in range(ML_NH):
                hsl = slice(h * ML_DH, (h + 1) * ML_DH)
                t = hf_ref[pl.ds(r0, CHUNK), hsl] + hb_ref[:, hsl]
                mu = jnp.mean(t, axis=-1, keepdims=True)
                var = jnp.mean(jnp.square(t - mu), axis=-1, keepdims=True)
                og = jax.nn.sigmoid(qkvo_ref[:, 3 * W_ML + h * ML_DH:3 * W_ML + (h + 1) * ML_DH])
                o_ref[:, hsl] = (og * ((t - mu) * lax.rsqrt(var + 1e-5) * nw_ref[:, hsl])).astype(BF16)

    _two_phase(nc, run)


def _mlstm_call(proj, gate_b, norm_w, rope_tabs, st0, base, n_seq, nc):
    init = st0 is not None
    rope = rope_tabs is not None
    L = nc * CHUNK
    if init:
        c0, n0, m0 = st0
    else:
        c0, n0, m0 = jnp.zeros((n_seq, 2, 1, 8, 128), F32), jnp.zeros((n_seq, 2, 8, 128), F32), jnp.zeros((n_seq, 2, ML_NH), F32)
    if rope:
        cos_t, sin_t = rope_tabs
        tab_spec = pl.BlockSpec((CHUNK, ML_DH), lambda b, j: (_chunk_of(j, nc), 0))
    else:
        cos_t = sin_t = jnp.zeros((8, 128), F32)
        tab_spec = pl.BlockSpec((8, 128), lambda b, j: (0, 0))
    gb_r, gb_c = gate_b.reshape(1, 4 * ML_NH), gate_b.reshape(4 * ML_NH, 1)

    def full(a):
        return pl.BlockSpec(a.shape, lambda b, j: (0,) * a.ndim)

    def per_seq(a):
        return pl.BlockSpec((None,) + a.shape[1:], lambda b, j: (b,) + (0,) * (a.ndim - 1))

    c_shape, n_shape = (ML_NH, ML_DH, ML_DH), (ML_NH, ML_DH)
    return pl.pallas_call(
        functools.partial(_mlstm_body, nc=nc, init=init, rope=rope),
        grid=(n_seq, 2 * nc),
        in_specs=[pl.BlockSpec((CHUNK, 4 * W_ML), _row_map(base, nc, O_QKVO // (4 * W_ML))),
                  pl.BlockSpec((CHUNK, W_SMALL), _row_map(base, nc, O_SMALL // W_SMALL)),
                  full(gb_r), full(gb_c), full(norm_w), tab_spec, tab_spec,
                  per_seq(c0), per_seq(n0), per_seq(m0)],
        out_specs=[pl.BlockSpec((CHUNK, W_ML), _out_map(nc)),
                   pl.BlockSpec((None, 2) + c_shape, lambda b, j: (b, 0, 0, 0, 0)),
                   pl.BlockSpec((None, 2) + n_shape, lambda b, j: (b, 0, 0, 0)),
                   pl.BlockSpec((None, 2, ML_NH), lambda b, j: (b, 0, 0))],
        out_shape=[jax.ShapeDtypeStruct((n_seq * L, W_ML), BF16),
                   jax.ShapeDtypeStruct((n_seq, 2) + c_shape, F32),
                   jax.ShapeDtypeStruct((n_seq, 2) + n_shape, F32),
                   jax.ShapeDtypeStruct((n_seq, 2, ML_NH), F32)],
        scratch_shapes=[pltpu.VMEM((L, W_ML), F32), pltpu.VMEM((CHUNK, W_ML), F32), pltpu.VMEM(c_shape, F32),
                        pltpu.VMEM((8, ML_DH), F32), pltpu.VMEM((8, 128), F32)],
        compiler_params=_cparams(("parallel", "arbitrary")),
        name="mlstm",
    )(proj, proj, gb_r, gb_c, norm_w, cos_t, sin_t, c0, n0, m0)


def _hy_filt_body(ef_ref, eb_ref, tf_ref, tb_ref, w1_ref, b1_ref, w2_ref, b2_ref, w3f_ref, w3b_ref,
                  dcf_ref, dcb_ref, fa_ref, fb_ref, nyq_ref, *, L):
    def mlp(e_ref, t_ref, w3_ref, dc_ref):
        f = jnp.sin(_dot(e_ref[...], w1_ref[...]) + b1_ref[...])
        f = jnp.sin(_dot(f.astype(BF16), w2_ref[...]) + b2_ref[...])
        return _dot(f.astype(BF16), w3_ref[...]) * jnp.exp(-dc_ref[...] * t_ref[...])

    hf = mlp(ef_ref, tf_ref, w3f_ref, dcf_ref)
    row = lax.broadcasted_iota(jnp.int32, hf.shape, 0)
    hb = jnp.where(row == 0, 0.0, mlp(eb_ref, tb_ref, w3b_ref, dcb_ref))
    norm = jnp.sum(jnp.abs(hf), axis=0, keepdims=True) + jnp.sum(jnp.abs(hb), axis=0, keepdims=True) + 1e-6
    hf, hb = hf / norm, hb / norm
    sgn = (1 - 2 * (row & 1)).astype(F32)
    nyq_ref[...] = jnp.sum(sgn * (hf + hb), axis=0, keepdims=True)
    fa_ref[...] = hf.astype(BF16)
    fb_ref[...] = hb.astype(BF16)


def _hy_filt_call(L, emb_f, emb_b, t_f, t_b, w1, b1, w2, b2, w3, decay):
    tc = 256
    nct = W_HY // tc

    def full(a):
        return pl.BlockSpec(a.shape, lambda o, c: (0, 0))

    def col(dirn):
        return lambda o, c: (0, (2 * o + dirn) * nct + c)

    out_spec = pl.BlockSpec((L, tc), lambda o, c: (0, o * nct + c))
    return pl.pallas_call(
        functools.partial(_hy_filt_body, L=L),
        grid=(HY_ORDER, nct),
        in_specs=[full(emb_f), full(emb_b), full(t_f), full(t_b), full(w1), full(b1), full(w2), full(b2),
                  pl.BlockSpec((HY_FH, tc), col(0)), pl.BlockSpec((HY_FH, tc), col(1)),
                  pl.BlockSpec((1, tc), col(0)), pl.BlockSpec((1, tc), col(1))],
        out_specs=[out_spec, out_spec, pl.BlockSpec((1, tc), lambda o, c: (0, o * nct + c))],
        out_shape=[jax.ShapeDtypeStruct((L, HY_ORDER * W_HY), BF16), jax.ShapeDtypeStruct((L, HY_ORDER * W_HY), BF16),
                   jax.ShapeDtypeStruct((1, HY_ORDER * W_HY), F32)],
        compiler_params=_cparams(("parallel", "parallel")),
        name="hy_filt",
    )(emb_f, emb_b, t_f, t_b, w1, b1, w2, b2, w3, w3, decay, decay)


def _hy_spec_body(fa_ref, fb_ref, cz_ref, sz_ref, fre_ref, fim_ref, *, kb):
    cz, sz = cz_ref[...], sz_ref[...]
    a, b = fa_ref[...], fb_ref[...]
    kk = pl.program_id(1) * kb + lax.broadcasted_iota(jnp.int32, (kb, 1), 0)
    sgn = (1 - 2 * (kk & 1)).astype(F32)
    fre_ref[...] = (_dot(cz, a) + sgn * _dot(cz, b)).astype(BF16)
    fim_ref[...] = (-(_dot(sz, a) + sgn * _dot(sz, b))).astype(BF16)


def _hy_spec_call(L, kb, fa, fb, cz, sz):
    tc = 256
    n = fa.shape[1]
    return pl.pallas_call(
        functools.partial(_hy_spec_body, kb=kb),
        grid=(n // tc, L // kb),
        in_specs=[pl.BlockSpec((L, tc), lambda c, k: (0, c)), pl.BlockSpec((L, tc), lambda c, k: (0, c)),
                  pl.BlockSpec((kb, L), lambda c, k: (k, 0)), pl.BlockSpec((kb, L), lambda c, k: (k, 0))],
        out_specs=[pl.BlockSpec((kb, tc), lambda c, k: (k, c)), pl.BlockSpec((kb, tc), lambda c, k: (k, c))],
        out_shape=[jax.ShapeDtypeStruct((L, n), BF16), jax.ShapeDtypeStruct((L, n), BF16)],
        compiler_params=_cparams(("parallel", "parallel")),
        name="hy_spec",
    )(fa, fb, cz, sz)


def _hy_conv_body(v_ref, x1_ref, x2_ref, fre_ref, fim_ref, nyq_ref, bias_ref, cz_ref, sz_ref, czt_ref, szt_ref,
                  o_ref, z_ref, zb_ref, acc_ref, *, L, kb, nkb):
    o, k = pl.program_id(2), pl.program_id(3)
    inv_n = 1.0 / (2 * L)

    @pl.when((o == 0) & (k == 0))
    def _():
        z_ref[...] = v_ref[...]

    @pl.when(k == 0)
    def _():
        zb_ref[...] = z_ref[...].astype(BF16)
        acc_ref[...] = jnp.zeros_like(acc_ref)

    zb = zb_ref[...]
    zc = _dot(cz_ref[...], zb)
    zs = _dot(sz_ref[...], zb)
    fre, fim = fre_ref[...].astype(F32), fim_ref[...].astype(F32)
    kk = k * kb + lax.broadcasted_iota(jnp.int32, (kb, 1), 0)
    wk = jnp.where(kk == 0, inv_n, 2.0 * inv_n)
    yre = ((zc * fre + zs * fim) * wk).astype(BF16)
    yim = ((zc * fim - zs * fre) * wk).astype(BF16)
    acc_ref[...] += _dot(czt_ref[...], yre) - _dot(szt_ref[...], yim)

    @pl.when(k == nkb - 1)
    def _():
        z = z_ref[...]
        row = lax.broadcasted_iota(jnp.int32, (L, 1), 0)
        sgn = (1 - 2 * (row & 1)).astype(F32)
        z_nyq = jnp.sum(z * sgn, axis=0, keepdims=True)
        y = acc_ref[...] + sgn * (z_nyq * nyq_ref[...] * inv_n) + bias_ref[...] * z

        @pl.when(o == 0)
        def _():
            z_ref[...] = x1_ref[...] * y

        @pl.when(o == 1)
        def _():
            o_ref[...] = (x2_ref[...] * y).astype(BF16)


def _hy_conv_call(hu, fre, fim, nyq, bias, cz, sz, L, kb, tc, base, n_seq):
    nct = W_HY // tc
    nkb = L // kb

    def zin(part):
        return pl.BlockSpec((L, tc), lambda b, c, o, k: (base + b, part * nct + c))

    fspec = pl.BlockSpec((kb, tc), lambda b, c, o, k: (k, o * nct + c))
    vspec = pl.BlockSpec((1, tc), lambda b, c, o, k: (0, o * nct + c))
    return pl.pallas_call(
        functools.partial(_hy_conv_body, L=L, kb=kb, nkb=nkb),
        grid=(n_seq, nct, HY_ORDER, nkb),
        in_specs=[zin(0), zin(1), zin(2), fspec, fspec, vspec, vspec,
                  pl.BlockSpec((kb, L), lambda b, c, o, k: (k, 0)), pl.BlockSpec((kb, L), lambda b, c, o, k: (k, 0)),
                  pl.BlockSpec((L, kb), lambda b, c, o, k: (0, k)), pl.BlockSpec((L, kb), lambda b, c, o, k: (0, k))],
        out_specs=pl.BlockSpec((L, tc), lambda b, c, o, k: (b, c)),
        out_shape=jax.ShapeDtypeStruct((n_seq * L, W_HY), BF16),
        scratch_shapes=[pltpu.VMEM((L, tc), F32), pltpu.VMEM((L, tc), BF16), pltpu.VMEM((L, tc), F32)],
        compiler_params=_cparams(("parallel", "parallel", "arbitrary", "arbitrary")),
        name="hy_conv",
    )(hu, hu, hu, fre, fim, nyq, bias, cz, sz, cz, sz)


def _hyena_tables(L):
    t = np.arange(L, dtype=np.float32) / np.float32(L)

    def emb(tt):
        ang = (np.float32(2.0 * math.pi) * tt[:, None] * np.arange(1, HY_BANDS + 1, dtype=np.float32)[None, :]).astype(np.float32)
        e = np.concatenate([tt[:, None], np.cos(ang), np.sin(ang)], axis=-1).astype(np.float32)
        return np.pad(e, ((0, 0), (0, 128 - HY_EMB)))

    tb = np.concatenate([t[:1], t[:0:-1]])
    kk = np.arange(L, dtype=np.int64)
    ang = (2.0 * math.pi / (2 * L)) * ((kk[:, None] * kk[None, :]) % (2 * L)).astype(np.float64)
    return dict(emb_f=jnp.asarray(emb(t), BF16), emb_b=jnp.asarray(emb(tb), BF16),
                t_f=jnp.asarray(t[:, None]), t_b=jnp.asarray(tb[:, None]),
                cz=jnp.asarray(np.cos(ang), BF16), sz=jnp.asarray(np.sin(ang), BF16))


def _rope_tables(L):
    nf = ML_DH // 4
    freqs = (np.float32(ROPE_BASE) ** (-np.arange(nf, dtype=np.float32) / np.float32(nf))).astype(np.float32)
    pos = np.arange(L)
    out_c, out_s = [], []
    for p in (pos // GRID_W, pos % GRID_W):
        ang = p.astype(np.float32)[:, None] * freqs[None, :]
        c, s = np.cos(ang).astype(np.float32), np.sin(ang).astype(np.float32)
        out_c += [c, c]
        out_s += [-s, s]
    return jnp.asarray(np.concatenate(out_c, axis=-1)), jnp.asarray(np.concatenate(out_s, axis=-1))


def kernel(x_prompt, x_sample, state_rglru, state_ssd, state_mlstm_C, state_mlstm_n, state_mlstm_m, c, c_ctx,
           ada_w, ada_b, ln_g, ln_b, ffn_wg, ffn_wu, ffn_wd, w_in, rg_conv_w, rg_conv_b, rg_gate_w, rg_gate_b,
           rg_lambda, hy_conv_w, hy_conv_b, hy_w1, hy_b1, hy_w2, hy_b2, hy_w3, hy_decay, hy_bias,
           ssd_conv_w, ssd_conv_b, ssd_dt_bias, ssd_A_log, ssd_D, ssd_norm_w, ml_gate_b, ml_norm_w,
           branch_w, mix_out):
    bp, lp_len = x_prompt.shape[:2]
    bs, ls = x_sample.shape[:2]
    mp, ms = bp * lp_len, bs * ls
    ncp, ncs = lp_len // CHUNK, ls // CHUNK

    def midx(i, bm):
        return jnp.where(i < mp // bm, 0, 1 + (i - mp // bm) // (ls // bm))

    cond = jnp.concatenate([c_ctx[None], c, jnp.zeros((8 - 1 - bs, D_MODEL), F32)], axis=0)
    s = jax.nn.silu(cond).astype(BF16)
    mod = _ada_call(s, ada_w, ada_b.reshape(DEPTH, 1, N_MOD * D_MODEL))
    mod = mod[:, :1 + bs].reshape(DEPTH, 1 + bs, N_MOD, 1, D_MODEL)

    def mod_row(l, j):
        return mod[l, :, j]

    o = IN_OFFSETS

    def wseg(a, b):
        return w_in[:, :, a:b]

    w_small = jnp.concatenate(
        [wseg(o[1], o[2]), wseg(0, o[1]), wseg(o[2], o[4]),
         wseg(o[4], o[5]), wseg(o[6], o[7]),
         jnp.zeros((DEPTH, D_MODEL, O_QKVO - O_SMALL - 2 * SSD_H - 4 * ML_NH), F32),
         wseg(o[5], o[6])], axis=-1).astype(BF16)
    w_merge = wseg(o[7], w_in.shape[-1]).astype(BF16)
    wg_b, wu_b, wd_b = ffn_wg.astype(BF16), ffn_wu.astype(BF16), ffn_wd.astype(BF16)
    bw_b, mo_b = branch_w.astype(BF16), mix_out.astype(BF16)
    gw_b = rg_gate_w.astype(BF16)
    hw1 = jnp.pad(hy_w1, ((0, 0), (0, 128 - HY_EMB), (0, 0))).astype(BF16)
    hw2, hw3 = hy_w2.astype(BF16), hy_w3.astype(BF16)

    tabs_p, tabs_s = _hyena_tables(lp_len), _hyena_tables(ls)
    rope_s = _rope_tables(ls)

    x = jnp.concatenate([x_prompt.reshape(mp, D_MODEL), x_sample.reshape(ms, D_MODEL)], axis=0)
    h = _modulate_call(x, mod_row(0, 0), mod_row(0, 1), midx)

    new_states = []
    for l in range(DEPTH):
        lng = ln_g[l].reshape(3, 1, D_MODEL)
        lnb = ln_b[l].reshape(3, 1, D_MODEL)

        a = _ffn_up_call(h, wg_b, wu_b, (l, 0))
        x, h = _res_ln_call(a, wd_b, (l, 0), x, mod_row(l, 2), lng[0], lnb[0], mod_row(l, 3), mod_row(l, 4),
                            midx, 0.5, True)

        proj = _in_proj_call(h, w_small, (l,))
        hu, xa, xbc = _conv_call(proj, hy_conv_w[l], hy_conv_b[l][None], rg_conv_w[l], rg_conv_b[l][None],
                                 ssd_conv_w[l], ssd_conv_b[l][None], mp // CHUNK, ncp, ncs)

        gb = rg_gate_b[l].reshape(4, W_RG)
        a_p, rg_last = _rglru_call(xa, proj, gw_b[l], gb, rg_lambda[l], None, 0, bp, ncp)
        a_s, _ = _rglru_call(xa, proj, gw_b[l], gb, rg_lambda[l], state_rglru[:, l], mp // CHUNK, bs, ncs)

        outs_b = []
        for tabs, L, kb, tc, base, n_seq in ((tabs_p, lp_len, lp_len, W_HY, 0, bp), (tabs_s, ls, 256, 256, mp // ls, bs)):
            fa, fb, nyq = _hy_filt_call(L, tabs['emb_f'], tabs['emb_b'], tabs['t_f'], tabs['t_b'], hw1[l], hy_b1[l][None],
                                        hw2[l], hy_b2[l][None], hw3[l], hy_decay[l][None])
            fre, fim = _hy_spec_call(L, kb, fa, fb, tabs['cz'], tabs['sz'])
            outs_b.append(_hy_conv_call(hu, fre, fim, nyq, hy_bias[l].reshape(1, HY_ORDER * W_HY), tabs['cz'], tabs['sz'],
                                        L, kb, tc, base, n_seq))

        dvec = jnp.repeat(ssd_D[l], SSD_P)[None]
        nw = ssd_norm_w[l][None]
        c_p, ssd_last = _ssd_call(xbc, proj, ssd_dt_bias[l], ssd_A_log[l], dvec, nw, None, 0, bp, ncp)
        c_s, _ = _ssd_call(xbc, proj, ssd_dt_bias[l], ssd_A_log[l], dvec, nw, state_ssd[:, l], mp // CHUNK, bs, ncs)

        mnw = ml_norm_w[l][None]
        d_p, cl, nl, ml = _mlstm_call(proj, ml_gate_b[l], mnw, None, None, 0, bp, ncp)
        d_s, _, _, _ = _mlstm_call(proj, ml_gate_b[l], mnw, rope_s,
                                   (state_mlstm_C[:, l], state_mlstm_n[:, l], state_mlstm_m[:, l]), mp // CHUNK, bs, ncs)
        new_states.append((rg_last, ssd_last, cl, nl, ml))

        merged = _merge_call(h, (a_p, outs_b[0], c_p, d_p), (a_s, outs_b[1], c_s, d_s), w_merge, bw_b, (l,))
        x, h = _res_ln_call(merged, mo_b, (l,), x, mod_row(l, 5), lng[1], lnb[1], mod_row(l, 6), mod_row(l, 7),
                            midx, 1.0, True)

        a = _ffn_up_call(h, wg_b, wu_b, (l, 1))
        last = l == DEPTH - 1
        nl_ = l if last else l + 1
        x, h = _res_ln_call(a, wd_b, (l, 1), x, mod_row(l, 8), lng[2], lnb[2], mod_row(nl_, 0), mod_row(nl_, 1),
                            midx, 0.5, not last)

    yp = x[:mp].reshape(bp, lp_len, D_MODEL)
    ys = x[mp:].reshape(bs, ls, D_MODEL)
    outs = tuple(jnp.stack([new_states[l][j] for l in range(DEPTH)], axis=1) for j in range(5))
    return (yp, ys) + outs
```

```python
import functools
import math

import jax
import jax.numpy as jnp
import numpy as np
from jax import lax
from jax.experimental import pallas as pl
from jax.experimental.pallas import tpu as pltpu

D_MODEL = 4096
DEPTH = 2
GRID_W = 64
N_BRANCH = 4
W_BRANCH = D_MODEL // N_BRANCH
D_FF = 2 * D_MODEL
N_MOD = 9
CHUNK = 128
W_RG = W_BRANCH
RG_BLOCKS = 8
RG_BW = W_RG // RG_BLOCKS
RG_C = 8.0
W_HY = W_BRANCH
HY_ORDER = 2
HY_BANDS = 16
HY_EMB = 2 * HY_BANDS + 1
HY_FH = 64
SSD_DI = W_BRANCH
SSD_P = 64
SSD_H = SSD_DI // SSD_P
SSD_N = 128
SSD_G = 2
SSD_CONV_CH = SSD_DI + 2 * SSD_G * SSD_N
W_ML = W_BRANCH
ML_NH = 4
ML_DH = W_ML // ML_NH
ROPE_BASE = 10000.0
ALPHA = (2 * DEPTH) ** 0.25

IN_SIZES = (W_RG, W_RG, 3 * W_HY, SSD_DI, SSD_CONV_CH, 2 * SSD_H, 4 * W_ML, 4 * ML_NH, N_BRANCH * D_MODEL)
IN_OFFSETS = tuple(int(s) for s in np.cumsum(IN_SIZES)[:-1])

F32 = jnp.float32
BF16 = jnp.bfloat16
HI = lax.Precision.HIGHEST
NT = (((1,), (1,)), ((), ()))
TN = (((0,), (0,)), ((), ()))

O_HY = 0
O_RGX = O_HY + 3 * W_HY
O_RGY = O_RGX + W_RG
O_SSDZ = O_RGY + W_RG
O_XBC = O_SSDZ + SSD_DI
O_SMALL = O_XBC + SSD_CONV_CH
W_SMALL = 128
N_PROJ = 8192

VMEM_LIMIT = 56 * 1024 * 1024


def _cparams(sem):
    return pltpu.CompilerParams(dimension_semantics=sem, vmem_limit_bytes=VMEM_LIMIT)


def _dot(a, b):
    return jnp.dot(a, b, preferred_element_type=F32)


def _dg(a, b, dims, precision=None):
    return lax.dot_general(a, b, dims, preferred_element_type=F32, precision=precision)


def _ada_body(s_ref, w_ref, b_ref, o_ref):
    o_ref[...] = _dot(s_ref[...], w_ref[...].astype(BF16)) + b_ref[...]


def _ada_call(s, ada_w, ada_b):
    bn = 512
    n = ada_w.shape[-1]
    return pl.pallas_call(
        _ada_body,
        grid=(DEPTH, n // bn),
        in_specs=[pl.BlockSpec((8, D_MODEL), lambda l, j: (0, 0)),
                  pl.BlockSpec((None, D_MODEL, bn), lambda l, j: (l, 0, j)),
                  pl.BlockSpec((None, 1, bn), lambda l, j: (l, 0, j))],
        out_specs=pl.BlockSpec((None, 8, bn), lambda l, j: (l, 0, j)),
        out_shape=jax.ShapeDtypeStruct((DEPTH, 8, n), F32),
        compiler_params=_cparams(("parallel", "parallel")),
        name="ada_mod",
    )(s, ada_w, ada_b)


def _modulate_body(x_ref, sh_ref, sc_ref, o_ref):
    o_ref[...] = (x_ref[...] * (1.0 + sc_ref[...]) + sh_ref[...]).astype(BF16)


def _modulate_call(x, shift, scale, midx):
    m = x.shape[0]
    bm = 256
    mspec = pl.BlockSpec((None, 1, D_MODEL), lambda i: (midx(i, bm), 0, 0))
    return pl.pallas_call(
        _modulate_body,
        grid=(m // bm,),
        in_specs=[pl.BlockSpec((bm, D_MODEL), lambda i: (i, 0)), mspec, mspec],
        out_specs=pl.BlockSpec((bm, D_MODEL), lambda i: (i, 0)),
        out_shape=jax.ShapeDtypeStruct((m, D_MODEL), BF16),
        compiler_params=_cparams(("parallel",)),
        name="modulate",
    )(x, shift, scale)


def _ffn_up_body(h_ref, wg_ref, wu_ref, o_ref):
    h = h_ref[...]
    g = _dot(h, wg_ref[...])
    u = _dot(h, wu_ref[...])
    o_ref[...] = (g * jax.nn.sigmoid(g) * u).astype(BF16)


def _wspec(lead, block, index_fn):
    lead = tuple(lead)
    return pl.BlockSpec((None,) * len(lead) + tuple(block), lambda *g: lead + tuple(index_fn(*g)))


def _side_plan(sides, grid):
    n_steps = grid[0] * grid[1]
    ins, outs, shapes = [], [], []
    for src, lead in sides:
        r, c = src.shape[-2:]
        rb = next(b for b in (16, 32, 64, 128, 256, 512) if r % b == 0 and r // b <= n_steps)
        nblk = r // rb

        def blk(i, j, nblk=nblk):
            return jnp.minimum(i * grid[1] + j, nblk - 1)

        ins.append(_wspec(lead, (rb, c), lambda i, j, blk=blk: (blk(i, j), 0)))
        outs.append(pl.BlockSpec((rb, c), lambda i, j, blk=blk: (blk(i, j), 0)))
        shapes.append(jax.ShapeDtypeStruct((r, c), BF16))
    return ins, outs, shapes


def _with_sides(body, n_in, n_out, n_side):
    def wrapped(*refs):
        ins, rest = refs[:n_in], refs[n_in:]
        side_src, rest = rest[:n_side], rest[n_side:]
        outs, rest = rest[:n_out], rest[n_out:]
        side_dst, scratch = rest[:n_side], rest[n_side:]
        body(*ins, *outs, *scratch)
        for s, d in zip(side_src, side_dst):
            d[...] = s[...].astype(BF16)
    return wrapped


def _ffn_up_call(h, wg, wu, lead, sides=()):
    m = h.shape[0]
    bm, bn = 1024, 512
    grid = (m // bm, D_FF // bn)
    s_in, s_out, s_shape = _side_plan(sides, grid)
    res = pl.pallas_call(
        _with_sides(_ffn_up_body, 3, 1, len(sides)),
        grid=grid,
        in_specs=[pl.BlockSpec((bm, D_MODEL), lambda i, j: (i, 0)),
                  _wspec(lead, (D_MODEL, bn), lambda i, j: (0, j)),
                  _wspec(lead, (D_MODEL, bn), lambda i, j: (0, j))] + s_in,
        out_specs=[pl.BlockSpec((bm, bn), lambda i, j: (i, j))] + s_out,
        out_shape=[jax.ShapeDtypeStruct((m, D_FF), BF16)] + s_shape,
        compiler_params=_cparams(("arbitrary", "arbitrary")),
        name="ffn_up",
    )(h, wg, wu, *[s for s, _ in sides])
    return res[0], res[1:]


def _in_proj_body(h_ref, w_ref, o_ref):
    o_ref[...] = _dot(h_ref[...], w_ref[...])


def _in_proj_call(h, w, lead, sides=()):
    m = h.shape[0]
    n = w.shape[-1]
    bm, bn = 1024, 512
    grid = (m // bm, n // bn)
    s_in, s_out, s_shape = _side_plan(sides, grid)
    res = pl.pallas_call(
        _with_sides(_in_proj_body, 2, 1, len(sides)),
        grid=grid,
        in_specs=[pl.BlockSpec((bm, D_MODEL), lambda i, j: (i, 0)),
                  _wspec(lead, (D_MODEL, bn), lambda i, j: (0, j))] + s_in,
        out_specs=[pl.BlockSpec((bm, bn), lambda i, j: (i, j))] + s_out,
        out_shape=[jax.ShapeDtypeStruct((m, n), F32)] + s_shape,
        compiler_params=_cparams(("arbitrary", "arbitrary")),
        name="in_proj",
    )(h, w, *[s for s, _ in sides])
    return res[0], res[1:]


def _shift_cast_body(a_ref, b_ref, o_ref, *, shift, width):
    x = jnp.concatenate([a_ref[...], b_ref[...]], axis=1)
    o_ref[...] = x[:, shift:shift + width].astype(BF16)


def _shift_cast_call(w, col0, ncols):
    depth, rows, _ = w.shape
    rb, width, lane = 1024, 512, 128
    shift = col0 % lane
    base = col0 - shift
    assert base % width == 0 and ncols % width == 0 and rows % rb == 0
    return pl.pallas_call(
        functools.partial(_shift_cast_body, shift=shift, width=width),
        grid=(depth, rows // rb, ncols // width),
        in_specs=[pl.BlockSpec((None, rb, width), lambda l, r, n: (l, r, base // width + n)),
                  pl.BlockSpec((None, rb, lane), lambda l, r, n: (l, r, (base + (n + 1) * width) // lane))],
        out_specs=pl.BlockSpec((None, rb, width), lambda l, r, n: (l, r, n)),
        out_shape=jax.ShapeDtypeStruct((depth, rows, ncols), BF16),
        compiler_params=_cparams(("parallel", "parallel", "parallel")),
        name="shift_cast",
    )(w, w)


def _merge_body(*refs, p_blocks):
    h_ref, bp, bs, wg, wb, o_ref = refs[0], refs[1:5], refs[5:9], refs[9:13], refs[13:17], refs[17]
    h = h_ref[...]
    is_prompt = pl.program_id(0) < p_blocks
    acc = None
    for j in range(N_BRANCH):
        br = jnp.where(is_prompt, bp[j][...], bs[j][...])
        t = jax.nn.sigmoid(_dot(h, wg[j][...])) * _dot(br, wb[j][...])
        acc = t if acc is None else acc + t
    o_ref[...] = acc.astype(BF16)


def _merge_call(h, br_prompt, br_sample, w_merge, lead, branch_w, sides=()):
    m = h.shape[0]
    bm, bn = 512, 256
    nb = D_MODEL // bn
    grid = (m // bm, nb)
    p_blocks = br_prompt[0].shape[0] // bm
    bp_specs = [pl.BlockSpec((bm, W_BRANCH), lambda i, n: (jnp.minimum(i, p_blocks - 1), 0)) for _ in range(N_BRANCH)]
    bs_specs = [pl.BlockSpec((bm, W_BRANCH), lambda i, n: (jnp.maximum(i - p_blocks, 0), 0)) for _ in range(N_BRANCH)]
    wg_specs = [_wspec(lead, (D_MODEL, bn), functools.partial(lambda i, n, j: (0, j * nb + n), j=j))
                for j in range(N_BRANCH)]
    wb_specs = [pl.BlockSpec((W_BRANCH, bn), functools.partial(lambda i, n, j: (j, n), j=j)) for j in range(N_BRANCH)]
    s_in, s_out, s_shape = _side_plan(sides, grid)
    res = pl.pallas_call(
        _with_sides(functools.partial(_merge_body, p_blocks=p_blocks), 1 + 4 * N_BRANCH, 1, len(sides)),
        grid=grid,
        in_specs=[pl.BlockSpec((bm, D_MODEL), lambda i, n: (i, 0))] + bp_specs + bs_specs + wg_specs + wb_specs + s_in,
        out_specs=[pl.BlockSpec((bm, bn), lambda i, n: (i, n))] + s_out,
        out_shape=[jax.ShapeDtypeStruct((m, D_MODEL), BF16)] + s_shape,
        compiler_params=_cparams(("arbitrary", "arbitrary")),
        name="merge",
    )(h, *br_prompt, *br_sample, *([w_merge] * N_BRANCH), *([branch_w] * N_BRANCH), *[s for s, _ in sides])
    return res[0], res[1:]


def _res_ln_body(a_ref, w_ref, x_ref, gate_ref, lng_ref, lnb_ref, nsh_ref, nsc_ref, y_ref, hn_ref, st_ref, *,
                 coef, nk, bm, emit_next):
    k = pl.program_id(1)

    @pl.when(k == 0)
    def _():
        y_ref[...] = _dot(a_ref[...], w_ref[...])

    @pl.when(k > 0)
    def _():
        y_ref[...] += _dot(a_ref[...], w_ref[...])

    @pl.when(k == nk - 1)
    def _():
        rows = 8
        gate = coef * gate_ref[...]
        lng, lnb = lng_ref[...], lnb_ref[...]
        nsc, nsh = 1.0 + nsc_ref[...], nsh_ref[...]

        def rows_of(c):
            return pl.ds(pl.multiple_of(c * rows, rows), rows)

        def sweep_mean(c, carry):
            rs = rows_of(c)
            r = ALPHA * x_ref[rs, :] + gate * y_ref[rs, :]
            y_ref[rs, :] = r
            st_ref[rs, 0:1] = jnp.mean(r, axis=-1, keepdims=True)
            return carry

        def sweep_var(c, carry):
            rs = rows_of(c)
            d = y_ref[rs, :] - st_ref[rs, 0:1]
            st_ref[rs, 1:2] = lax.rsqrt(jnp.mean(d * d, axis=-1, keepdims=True) + 1e-5)
            return carry

        def sweep_out(c, carry):
            rs = rows_of(c)
            o = (y_ref[rs, :] - st_ref[rs, 0:1]) * st_ref[rs, 1:2] * lng + lnb
            y_ref[rs, :] = o
            if emit_next:
                hn_ref[rs, :] = (o * nsc + nsh).astype(BF16)
            return carry

        for sweep, unroll in ((sweep_mean, 8), (sweep_var, 8), (sweep_out, 4)):
            lax.fori_loop(0, bm // rows, sweep, 0, unroll=unroll)
        if not emit_next:
            hn_ref[...] = jnp.zeros_like(hn_ref)


def _res_ln_call(a, w, lead, x, gate, lng, lnb, nshift, nscale, midx, coef, emit_next):
    m, kdim = a.shape
    bm, bk = 512, 512
    nk = kdim // bk
    hn_rows = bm if emit_next else 8
    mspec = pl.BlockSpec((None, 1, D_MODEL), lambda i, k: (midx(i, bm), 0, 0))
    vspec = pl.BlockSpec((1, D_MODEL), lambda i, k: (0, 0))
    y, hn = pl.pallas_call(
        functools.partial(_res_ln_body, coef=coef, nk=nk, bm=bm, emit_next=emit_next),
        grid=(m // bm, nk),
        in_specs=[pl.BlockSpec((bm, bk), lambda i, k: (i, k)),
                  _wspec(lead, (bk, D_MODEL), lambda i, k: (k, 0)),
                  pl.BlockSpec((bm, D_MODEL), lambda i, k: (i, 0)),
                  mspec, vspec, vspec, mspec, mspec],
        out_specs=[pl.BlockSpec((bm, D_MODEL), lambda i, k: (i, 0)),
                   pl.BlockSpec((hn_rows, D_MODEL), lambda i, k: (i, 0))],
        out_shape=[jax.ShapeDtypeStruct((m, D_MODEL), F32),
                   jax.ShapeDtypeStruct((m if emit_next else 8 * (m // bm), D_MODEL), BF16)],
        scratch_shapes=[pltpu.VMEM((bm, 128), F32)],
        compiler_params=_cparams(("parallel", "arbitrary")),
        name="res_ln",
    )(a, w, x, gate, lng, lnb, nshift, nscale)
    return y, hn


def _conv_piece(pad_ref, w_ref, b_ref, o_ref, c0, width, taps, pad_l, act):
    cs = slice(c0, c0 + width)
    acc = b_ref[:, cs]
    for j in range(taps):
        r0 = 8 - pad_l + j
        acc = acc + w_ref[j:j + 1, cs] * pad_ref[r0:r0 + CHUNK, cs]
    o_ref[:, cs] = act(acc)


def _conv_body(hy_p, hy_c, hy_n, rg_p, rg_c, rg_n, xb_p, xb_c, xb_n,
               hw_ref, hb_ref, rw_ref, rb_ref, xw_ref, xb_ref,
               hu_ref, xa_ref, xbc_ref, pad_ref, *, chunk_pos):
    first, last = chunk_pos(pl.program_id(0))
    for prev, cur, nxt, w_ref, b_ref, o_ref, width, taps, pad_l, act in (
            (hy_p, hy_c, hy_n, hw_ref, hb_ref, hu_ref, 3 * W_HY, 3, 1, lambda v: v),
            (rg_p, rg_c, rg_n, rw_ref, rb_ref, xa_ref, W_RG, 4, 2, lambda v: v),
            (xb_p, xb_c, xb_n, xw_ref, xb_ref, xbc_ref, SSD_CONV_CH, 4, 2, jax.nn.silu)):
        pad_ref[0:8, 0:width] = jnp.where(first, 0.0, prev[...])
        pad_ref[8:8 + CHUNK, 0:width] = cur[...]
        pad_ref[8 + CHUNK:16 + CHUNK, 0:width] = jnp.where(last, 0.0, nxt[...])
        for c0 in range(0, width, 512):
            _conv_piece(pad_ref, w_ref, b_ref, o_ref, c0, min(512, width - c0), taps, pad_l, act)


def _conv_call(proj, hy_w, hy_b, rg_w, rg_b, xb_w, xb_b, mp_chunks, ncp, ncs):
    m = proj.shape[0]
    n8 = m // 8

    def chunk_pos(i):
        pos = jnp.where(i < mp_chunks, i % ncp, (i - mp_chunks) % ncs)
        n = jnp.where(i < mp_chunks, ncp, ncs)
        return pos == 0, pos == n - 1

    def trio(width, col):
        return [pl.BlockSpec((8, width), lambda i: (jnp.maximum(i * (CHUNK // 8) - 1, 0), col)),
                pl.BlockSpec((CHUNK, width), lambda i: (i, col)),
                pl.BlockSpec((8, width), lambda i: (jnp.minimum((i + 1) * (CHUNK // 8), n8 - 1), col))]

    def full(a):
        return pl.BlockSpec(a.shape, lambda i: (0, 0))

    params = (hy_w, hy_b, rg_w, rg_b, xb_w, xb_b)
    return pl.pallas_call(
        functools.partial(_conv_body, chunk_pos=chunk_pos),
        grid=(m // CHUNK,),
        in_specs=trio(3 * W_HY, O_HY // (3 * W_HY)) + trio(W_RG, O_RGX // W_RG)
        + trio(SSD_CONV_CH, O_XBC // SSD_CONV_CH) + [full(a) for a in params],
        out_specs=[pl.BlockSpec((CHUNK, 3 * W_HY), lambda i: (i, 0)),
                   pl.BlockSpec((CHUNK, W_RG), lambda i: (i, 0)),
                   pl.BlockSpec((CHUNK, SSD_CONV_CH), lambda i: (i, 0))],
        out_shape=[jax.ShapeDtypeStruct((m, 3 * W_HY), F32),
                   jax.ShapeDtypeStruct((m, W_RG), F32),
                   jax.ShapeDtypeStruct((m, SSD_CONV_CH), F32)],
        scratch_shapes=[pltpu.VMEM((CHUNK + 16, 3 * W_HY), F32)],
        compiler_params=_cparams(("parallel",)),
        name="dw_conv",
    )(*([proj] * 9), *params)


def _chunk_of(j, nc):
    return jnp.where(j < nc, j, 2 * nc - 1 - j)


def _row_map(base, nc, col):
    return lambda b, j: (base + b * nc + _chunk_of(j, nc), col)


def _out_map(nc):
    return lambda b, j: (b * nc + jnp.where(j < nc, nc - 1, 2 * nc - 1 - j), 0)


def _tri(rev):
    r = lax.broadcasted_iota(jnp.int32, (CHUNK, CHUNK), 0)
    c = lax.broadcasted_iota(jnp.int32, (CHUNK, CHUNK), 1)
    return (c >= r) if rev else (c <= r)


def _two_phase(nc, run):
    j = pl.program_id(1)

    @pl.when(j < nc)
    def _():
        run(0, j == 0, j == nc - 1, j)

    @pl.when(j >= nc)
    def _():
        run(1, j == nc, j == 2 * nc - 1, 2 * nc - 1 - j)


def _rglru_body(xa_ref, y_ref, gw_ref, gb_ref, lam_ref, h0_ref, o_ref, hl_ref,
                yf_ref, a_ref, u_ref, hb_ref, hc_ref, *, nc, init):
    def run(d, is_first, is_last, ci):
        x = xa_ref[...]
        xb = x.astype(BF16)
        sp = jax.nn.softplus(-lam_ref[d:d + 1, :])
        for n in range(RG_BLOCKS):
            cs = slice(n * RG_BW, (n + 1) * RG_BW)
            r = jax.nn.sigmoid(_dot(xb[:, cs], gw_ref[d, 0, n]) + gb_ref[2 * d:2 * d + 1, cs])
            i = jax.nn.sigmoid(_dot(xb[:, cs], gw_ref[d, 1, n]) + gb_ref[2 * d + 1:2 * d + 2, cs])
            log_a = -RG_C * r * sp[:, cs]
            th = jnp.tanh(log_a)
            a_ref[:, cs] = jnp.exp(log_a)
            u_ref[:, cs] = jnp.sqrt(-2.0 * th / (1.0 - th)) * (i * x[:, cs])

        @pl.when(is_first)
        def _():
            hc_ref[0:1, :] = h0_ref[d:d + 1, :] if init else jnp.zeros((1, W_RG), F32)

        def step(s, h):
            t = (CHUNK - 1 - s) if d else s
            h = a_ref[pl.ds(t, 1), :] * h + u_ref[pl.ds(t, 1), :]
            hb_ref[pl.ds(t, 1), :] = h
            return h

        h = lax.fori_loop(0, CHUNK, step, hc_ref[0:1, :], unroll=8)
        hc_ref[0:1, :] = h

        @pl.when(is_last)
        def _():
            hl_ref[d:d + 1, :] = h

        r0 = pl.multiple_of(ci * CHUNK, CHUNK)
        if d == 0:
            yf_ref[pl.ds(r0, CHUNK), :] = hb_ref[...]
        else:
            o_ref[...] = ((yf_ref[pl.ds(r0, CHUNK), :] + hb_ref[...]) * jax.nn.gelu(y_ref[...])).astype(BF16)

    _two_phase(nc, run)


def _rglru_call(xa, proj, gw, gb, lam, h0, base, n_seq, nc):
    init = h0 is not None
    if not init:
        h0 = jnp.zeros((n_seq, 2, W_RG), F32)
    L = nc * CHUNK
    st_spec = pl.BlockSpec((None, 2, W_RG), lambda b, j: (b, 0, 0))
    return pl.pallas_call(
        functools.partial(_rglru_body, nc=nc, init=init),
        grid=(n_seq, 2 * nc),
        in_specs=[pl.BlockSpec((CHUNK, W_RG), _row_map(base, nc, 0)),
                  pl.BlockSpec((CHUNK, W_RG), _row_map(base, nc, O_RGY // W_RG)),
                  pl.BlockSpec(gw.shape, lambda b, j: (0, 0, 0, 0, 0)),
                  pl.BlockSpec(gb.shape, lambda b, j: (0, 0)),
                  pl.BlockSpec(lam.shape, lambda b, j: (0, 0)),
                  st_spec],
        out_specs=[pl.BlockSpec((CHUNK, W_RG), _out_map(nc)), st_spec],
        out_shape=[jax.ShapeDtypeStruct((n_seq * L, W_RG), BF16),
                   jax.ShapeDtypeStruct((n_seq, 2, W_RG), F32)],
        scratch_shapes=[pltpu.VMEM((L, W_RG), F32), pltpu.VMEM((CHUNK, W_RG), F32), pltpu.VMEM((CHUNK, W_RG), F32),
                        pltpu.VMEM((CHUNK, W_RG), F32), pltpu.VMEM((8, W_RG), F32)],
        compiler_params=_cparams(("parallel", "arbitrary")),
        name="rglru",
    )(xa, proj, gw, gb, lam, h0)


def _ssd_body(xbc_ref, z_ref, sm_ref, dtb_r, dtb_c, al_r, al_c, dvec_ref, nw_ref, s0_ref, o_ref, sl_ref,
              yf_ref, yb_ref, s_ref, *, nc, init):
    def run(d, is_first, is_last, ci):
        hs = slice(SSD_H * d, SSD_H * (d + 1))
        sm = sm_ref[...]
        smt = sm.T
        dt_c = jax.nn.softplus(sm[:, hs] + dtb_r[:, hs])
        dt_r = jax.nn.softplus(smt[hs, :] + dtb_c[hs, :])
        mask = _tri(d)
        tri = jnp.where(mask, 1.0, 0.0).astype(F32)
        cs_c = jnp.dot(tri, dt_c * -jnp.exp(al_r[:, hs]), preferred_element_type=F32, precision=HI)
        cs_r = _dg(dt_r * -jnp.exp(al_c[hs, :]), tri, NT, HI)
        e = 0 if d else CHUNK - 1
        tot = cs_c[e:e + 1, :]
        w_end = jnp.exp(tot - cs_c) * dt_c
        ecs = jnp.exp(cs_c)
        etot = jnp.exp(tot)

        @pl.when(is_first)
        def _():
            s_ref[...] = s0_ref[d] if init else jnp.zeros(s_ref.shape, F32)

        xs = xbc_ref[:, 0:SSD_DI]
        xsb = xs.astype(BF16)
        for g in range(SSD_G):
            bb = xbc_ref[:, SSD_DI + g * SSD_N:SSD_DI + (g + 1) * SSD_N].astype(BF16)
            cb = xbc_ref[:, SSD_DI + (SSD_G + g) * SSD_N:SSD_DI + (SSD_G + g + 1) * SSD_N].astype(BF16)
            gmat = _dg(cb, bb, NT)
            for hh in range(SSD_H // SSD_G):
                h = g * (SSD_H // SSD_G) + hh
                ps = slice(h * SSD_P, (h + 1) * SSD_P)
                seg = cs_c[:, h:h + 1] - cs_r[h:h + 1, :]
                dec = jnp.exp(jnp.where(mask, seg, -jnp.inf))
                mh = (gmat * dec * dt_r[h:h + 1, :]).astype(BF16)
                sh = s_ref[h]
                yb_ref[:, ps] = _dot(mh, xsb[:, ps]) + _dg(cb, sh.astype(BF16), NT) * ecs[:, h:h + 1]
                xw = (xs[:, ps] * w_end[:, h:h + 1]).astype(BF16)
                s_ref[h] = etot[:, h:h + 1] * sh + _dg(xw, bb, TN)

        @pl.when(is_last)
        def _():
            sl_ref[d] = s_ref[...]

        r0 = pl.multiple_of(ci * CHUNK, CHUNK)
        if d == 0:
            yf_ref[pl.ds(r0, CHUNK), :] = yb_ref[...]
        else:
            yc = yf_ref[pl.ds(r0, CHUNK), :] + yb_ref[...] + dvec_ref[...] * xs
            v = yc * jax.nn.silu(z_ref[...])
            o_ref[...] = (v * lax.rsqrt(jnp.mean(v * v, axis=-1, keepdims=True) + 1e-6) * nw_ref[...]).astype(BF16)

    _two_phase(nc, run)


def _ssd_call(xbc, proj, dt_bias, a_log, dvec, norm_w, s0, base, n_seq, nc):
    init = s0 is not None
    if not init:
        s0 = jnp.zeros((n_seq, 2, 8, 8, SSD_N), F32)
    L = nc * CHUNK
    st_shape = (SSD_H, SSD_P, SSD_N)
    dtb_r, al_r = dt_bias.reshape(1, 2 * SSD_H), a_log.reshape(1, 2 * SSD_H)
    dtb_c, al_c = dt_bias.reshape(2 * SSD_H, 1), a_log.reshape(2 * SSD_H, 1)

    def full(a):
        return pl.BlockSpec(a.shape, lambda b, j: (0,) * a.ndim)

    s0_spec = pl.BlockSpec((None,) + s0.shape[1:], lambda b, j: (b, 0, 0, 0, 0))
    return pl.pallas_call(
        functools.partial(_ssd_body, nc=nc, init=init),
        grid=(n_seq, 2 * nc),
        in_specs=[pl.BlockSpec((CHUNK, SSD_CONV_CH), _row_map(base, nc, 0)),
                  pl.BlockSpec((CHUNK, SSD_DI), _row_map(base, nc, O_SSDZ // SSD_DI)),
                  pl.BlockSpec((CHUNK, W_SMALL), _row_map(base, nc, O_SMALL // W_SMALL)),
                  full(dtb_r), full(dtb_c), full(al_r), full(al_c), full(dvec), full(norm_w), s0_spec],
        out_specs=[pl.BlockSpec((CHUNK, SSD_DI), _out_map(nc)),
                   pl.BlockSpec((None, 2) + st_shape, lambda b, j: (b, 0, 0, 0, 0))],
        out_shape=[jax.ShapeDtypeStruct((n_seq * L, SSD_DI), BF16),
                   jax.ShapeDtypeStruct((n_seq, 2) + st_shape, F32)],
        scratch_shapes=[pltpu.VMEM((L, SSD_DI), F32), pltpu.VMEM((CHUNK, SSD_DI), F32), pltpu.VMEM(st_shape, F32)],
        compiler_params=_cparams(("parallel", "arbitrary")),
        name="ssd",
    )(xbc, proj, proj, dtb_r, dtb_c, al_r, al_c, dvec, norm_w, s0)


def _mlstm_body(qkvo_ref, sm_ref, gb_r, gb_c, nw_ref, cos_ref, sin_ref, c0_ref, n0_ref, m0_ref,
                o_ref, cl_ref, nl_ref, ml_ref, hf_ref, hb_ref, c_ref, n_ref, m_ref, *, nc, init, rope):
    def run(d, is_first, is_last, ci):
        sm = sm_ref[...]
        smt = sm.T
        o_i, o_f = 2 * SSD_H + 8 * d, 2 * SSD_H + 8 * d + ML_NH
        i_c = sm[:, o_i:o_i + ML_NH] + gb_r[:, 8 * d:8 * d + ML_NH]
        i_r = smt[o_i:o_i + ML_NH, :] + gb_c[8 * d:8 * d + ML_NH, :]
        lf_c = jax.nn.log_sigmoid(sm[:, o_f:o_f + ML_NH] + gb_r[:, 8 * d + ML_NH:8 * d + 2 * ML_NH])
        lf_r = jax.nn.log_sigmoid(smt[o_f:o_f + ML_NH, :] + gb_c[8 * d + ML_NH:8 * d + 2 * ML_NH, :])
        mask = _tri(d)
        tri = jnp.where(mask, 1.0, 0.0).astype(F32)
        b_c = jnp.dot(tri, lf_c, preferred_element_type=F32, precision=HI)
        b_r = _dg(lf_r, tri, NT, HI)
        e = 0 if d else CHUNK - 1
        b_end = b_c[e:e + 1, :]

        @pl.when(is_first)
        def _():
            if init:
                c_ref[...] = c0_ref[d]
                n_ref[0:ML_NH, :] = n0_ref[d]
                m_ref[0:1, 0:ML_NH] = m0_ref[d:d + 1, :]
            else:
                c_ref[...] = jnp.zeros(c_ref.shape, F32)
                n_ref[...] = jnp.zeros(n_ref.shape, F32)
                m_ref[...] = jnp.zeros(m_ref.shape, F32)

        def rot(t):
            if not rope:
                return t
            parts = []
            for p in range(ML_DH // 128):
                tp = t[:, p * 128:(p + 1) * 128]
                parts.append(tp * cos_ref[:, p * 128:(p + 1) * 128]
                             + pltpu.roll(tp, 64, axis=1) * sin_ref[:, p * 128:(p + 1) * 128])
            return jnp.concatenate(parts, axis=-1)

        for h in range(ML_NH):
            hsl = slice(h * ML_DH, (h + 1) * ML_DH)
            q = rot(qkvo_ref[:, h * ML_DH:(h + 1) * ML_DH])
            k = rot(qkvo_ref[:, W_ML + h * ML_DH:W_ML + (h + 1) * ML_DH] * (ML_DH ** -0.5))
            v = qkvo_ref[:, 2 * W_ML + h * ML_DH:2 * W_ML + (h + 1) * ML_DH]
            qb, kb = q.astype(BF16), k.astype(BF16)
            m_prev = m_ref[0:1, h:h + 1]
            dmat = jnp.where(mask, b_c[:, h:h + 1] - b_r[h:h + 1, :] + i_r[h:h + 1, :], -jnp.inf)
            g0 = b_c[:, h:h + 1] + m_prev
            mt = jnp.maximum(g0, jnp.max(dmat, axis=1, keepdims=True))
            w = jnp.exp(dmat - mt)
            w0 = jnp.exp(g0 - mt)
            s = _dg(qb, kb, NT) * w
            ch = c_ref[h]
            nh = n_ref[h:h + 1, :]
            num = _dot(s.astype(BF16), v.astype(BF16)) + w0 * _dg(qb, ch.astype(BF16), NT)
            den = jnp.sum(s, axis=1, keepdims=True) + w0 * jnp.sum(q * nh, axis=1, keepdims=True)
            hb_ref[:, hsl] = num / jnp.maximum(jnp.abs(den), jnp.exp(-mt))
            d_end = b_end[:, h:h + 1] - b_c[:, h:h + 1] + i_c[:, h:h + 1]
            g0e = b_end[:, h:h + 1] + m_prev
            m_new = jnp.maximum(g0e, jnp.max(d_end, axis=0, keepdims=True))
            we = jnp.exp(d_end - m_new)
            w0e = jnp.exp(g0e - m_new)
            c_ref[h] = w0e * ch + _dg((v * we).astype(BF16), kb, TN)
            n_ref[h:h + 1, :] = w0e * nh + jnp.sum(we * k, axis=0, keepdims=True)
            m_ref[0:1, h:h + 1] = m_new

        @pl.when(is_last)
        def _():
            cl_ref[d] = c_ref[...]
            nl_ref[d] = n_ref[0:ML_NH, :]
            ml_ref[d:d + 1, :] = m_ref[0:1, 0:ML_NH]

        r0 = pl.multiple_of(ci * CHUNK, CHUNK)
        if d == 0:
            hf_ref[pl.ds(r0, CHUNK), :] = hb_ref[...]
        else:
            for h in range(ML_NH):
                hsl = slice(h * ML_DH, (h + 1) * ML_DH)
                t = hf_ref[pl.ds(r0, CHUNK), hsl] + hb_ref[:, hsl]
                mu = jnp.mean(t, axis=-1, keepdims=True)
                var = jnp.mean(jnp.square(t - mu), axis=-1, keepdims=True)
                og = jax.nn.sigmoid(qkvo_ref[:, 3 * W_ML + h * ML_DH:3 * W_ML + (h + 1) * ML_DH])
                o_ref[:, hsl] = (og * ((t - mu) * lax.rsqrt(var + 1e-5) * nw_ref[:, hsl])).astype(BF16)

    _two_phase(nc, run)


def _mlstm_call(proj, proj_q, gate_b, norm_w, rope_tabs, st0, base, n_seq, nc):
    init = st0 is not None
    rope = rope_tabs is not None
    L = nc * CHUNK
    if init:
        c0, n0, m0 = st0
    else:
        c0, n0, m0 = jnp.zeros((n_seq, 2, 1, 8, 128), F32), jnp.zeros((n_seq, 2, 8, 128), F32), jnp.zeros((n_seq, 2, ML_NH), F32)
    if rope:
        cos_t, sin_t = rope_tabs
        tab_spec = pl.BlockSpec((CHUNK, ML_DH), lambda b, j: (_chunk_of(j, nc), 0))
    else:
        cos_t = sin_t = jnp.zeros((8, 128), F32)
        tab_spec = pl.BlockSpec((8, 128), lambda b, j: (0, 0))
    gb_r, gb_c = gate_b.reshape(1, 4 * ML_NH), gate_b.reshape(4 * ML_NH, 1)

    def full(a):
        return pl.BlockSpec(a.shape, lambda b, j: (0,) * a.ndim)

    def per_seq(a):
        return pl.BlockSpec((None,) + a.shape[1:], lambda b, j: (b,) + (0,) * (a.ndim - 1))

    c_shape, n_shape = (ML_NH, ML_DH, ML_DH), (ML_NH, ML_DH)
    return pl.pallas_call(
        functools.partial(_mlstm_body, nc=nc, init=init, rope=rope),
        grid=(n_seq, 2 * nc),
        in_specs=[pl.BlockSpec((CHUNK, 4 * W_ML), _row_map(base, nc, 0)),
                  pl.BlockSpec((CHUNK, W_SMALL), _row_map(base, nc, O_SMALL // W_SMALL)),
                  full(gb_r), full(gb_c), full(norm_w), tab_spec, tab_spec,
                  per_seq(c0), per_seq(n0), per_seq(m0)],
        out_specs=[pl.BlockSpec((CHUNK, W_ML), _out_map(nc)),
                   pl.BlockSpec((None, 2) + c_shape, lambda b, j: (b, 0, 0, 0, 0)),
                   pl.BlockSpec((None, 2) + n_shape, lambda b, j: (b, 0, 0, 0)),
                   pl.BlockSpec((None, 2, ML_NH), lambda b, j: (b, 0, 0))],
        out_shape=[jax.ShapeDtypeStruct((n_seq * L, W_ML), BF16),
                   jax.ShapeDtypeStruct((n_seq, 2) + c_shape, F32),
                   jax.ShapeDtypeStruct((n_seq, 2) + n_shape, F32),
                   jax.ShapeDtypeStruct((n_seq, 2, ML_NH), F32)],
        scratch_shapes=[pltpu.VMEM((L, W_ML), F32), pltpu.VMEM((CHUNK, W_ML), F32), pltpu.VMEM(c_shape, F32),
                        pltpu.VMEM((8, ML_DH), F32), pltpu.VMEM((8, 128), F32)],
        compiler_params=_cparams(("parallel", "arbitrary")),
        name="mlstm",
    )(proj_q, proj, gb_r, gb_c, norm_w, cos_t, sin_t, c0, n0, m0)


def _hy_filt_body(ef_ref, eb_ref, tf_ref, tb_ref, w1_ref, b1_ref, w2_ref, b2_ref, w3f_ref, w3b_ref,
                  dcf_ref, dcb_ref, fa_ref, fb_ref, nyq_ref, hidf_ref, hidb_ref, *, L):
    @pl.when((pl.program_id(0) == 0) & (pl.program_id(1) == 0))
    def _():
        for e_ref, hid_ref in ((ef_ref, hidf_ref), (eb_ref, hidb_ref)):
            f = jnp.sin(_dot(e_ref[...], w1_ref[...]) + b1_ref[...])
            hid_ref[...] = jnp.sin(_dot(f.astype(BF16), w2_ref[...]) + b2_ref[...]).astype(BF16)

    def mlp(hid_ref, t_ref, w3_ref, dc_ref):
        return _dot(hid_ref[...], w3_ref[...]) * jnp.exp(-dc_ref[...] * t_ref[...])

    hf = mlp(hidf_ref, tf_ref, w3f_ref, dcf_ref)
    row = lax.broadcasted_iota(jnp.int32, hf.shape, 0)
    hb = jnp.where(row == 0, 0.0, mlp(hidb_ref, tb_ref, w3b_ref, dcb_ref))
    norm = jnp.sum(jnp.abs(hf), axis=0, keepdims=True) + jnp.sum(jnp.abs(hb), axis=0, keepdims=True) + 1e-6
    hf, hb = hf / norm, hb / norm
    sgn = (1 - 2 * (row & 1)).astype(F32)
    nyq_ref[...] = jnp.sum(sgn * (hf + hb), axis=0, keepdims=True)
    fa_ref[...] = hf.astype(BF16)
    fb_ref[...] = hb.astype(BF16)


def _hy_filt_call(L, emb_f, emb_b, t_f, t_b, w1, b1, w2, b2, w3, decay):
    tc = 256
    nct = W_HY // tc

    def full(a):
        return pl.BlockSpec(a.shape, lambda o, c: (0, 0))

    def col(dirn):
        return lambda o, c: (0, (2 * o + dirn) * nct + c)

    out_spec = pl.BlockSpec((L, tc), lambda o, c: (0, o * nct + c))
    return pl.pallas_call(
        functools.partial(_hy_filt_body, L=L),
        grid=(HY_ORDER, nct),
        in_specs=[full(emb_f), full(emb_b), full(t_f), full(t_b), full(w1), full(b1), full(w2), full(b2),
                  pl.BlockSpec((HY_FH, tc), col(0)), pl.BlockSpec((HY_FH, tc), col(1)),
                  pl.BlockSpec((1, tc), col(0)), pl.BlockSpec((1, tc), col(1))],
        out_specs=[out_spec, out_spec, pl.BlockSpec((1, tc), lambda o, c: (0, o * nct + c))],
        out_shape=[jax.ShapeDtypeStruct((L, HY_ORDER * W_HY), BF16), jax.ShapeDtypeStruct((L, HY_ORDER * W_HY), BF16),
                   jax.ShapeDtypeStruct((1, HY_ORDER * W_HY), F32)],
        scratch_shapes=[pltpu.VMEM((L, HY_FH), BF16), pltpu.VMEM((L, HY_FH), BF16)],
        compiler_params=_cparams(("arbitrary", "arbitrary")),
        name="hy_filt",
    )(emb_f, emb_b, t_f, t_b, w1, b1, w2, b2, w3, w3, decay, decay)


def _hy_spec_body(fa_ref, fb_ref, cz_ref, sz_ref, fre_ref, fim_ref, *, kb):
    cz, sz = cz_ref[...], sz_ref[...]
    a, b = fa_ref[...], fb_ref[...]
    kk = pl.program_id(1) * kb + lax.broadcasted_iota(jnp.int32, (kb, 1), 0)
    sgn = (1 - 2 * (kk & 1)).astype(F32)
    fre_ref[...] = (_dot(cz, a) + sgn * _dot(cz, b)).astype(BF16)
    fim_ref[...] = (-(_dot(sz, a) + sgn * _dot(sz, b))).astype(BF16)


def _hy_spec_call(L, kb, fa, fb, cz, sz):
    tc = 256
    n = fa.shape[1]
    return pl.pallas_call(
        functools.partial(_hy_spec_body, kb=kb),
        grid=(n // tc, L // kb),
        in_specs=[pl.BlockSpec((L, tc), lambda c, k: (0, c)), pl.BlockSpec((L, tc), lambda c, k: (0, c)),
                  pl.BlockSpec((kb, L), lambda c, k: (k, 0)), pl.BlockSpec((kb, L), lambda c, k: (k, 0))],
        out_specs=[pl.BlockSpec((kb, tc), lambda c, k: (k, c)), pl.BlockSpec((kb, tc), lambda c, k: (k, c))],
        out_shape=[jax.ShapeDtypeStruct((L, n), BF16), jax.ShapeDtypeStruct((L, n), BF16)],
        compiler_params=_cparams(("parallel", "parallel")),
        name="hy_spec",
    )(fa, fb, cz, sz)


def _hy_conv_body(v_ref, x1_ref, x2_ref, fre_ref, fim_ref, nyq_ref, bias_ref, cz_ref, sz_ref, czt_ref, szt_ref,
                  o_ref, z_ref, zb_ref, acc_ref, *, L, kb, nkb):
    o, k = pl.program_id(2), pl.program_id(3)
    inv_n = 1.0 / (2 * L)

    @pl.when((o == 0) & (k == 0))
    def _():
        z_ref[...] = v_ref[...]

    @pl.when(k == 0)
    def _():
        zb_ref[...] = z_ref[...].astype(BF16)
        acc_ref[...] = jnp.zeros_like(acc_ref)

    zb = zb_ref[...]
    zc = _dot(cz_ref[...], zb)
    zs = _dot(sz_ref[...], zb)
    fre, fim = fre_ref[...].astype(F32), fim_ref[...].astype(F32)
    kk = k * kb + lax.broadcasted_iota(jnp.int32, (kb, 1), 0)
    wk = jnp.where(kk == 0, inv_n, 2.0 * inv_n)
    yre = ((zc * fre + zs * fim) * wk).astype(BF16)
    yim = ((zc * fim - zs * fre) * wk).astype(BF16)
    acc_ref[...] += _dot(czt_ref[...], yre) - _dot(szt_ref[...], yim)

    @pl.when(k == nkb - 1)
    def _():
        z = z_ref[...]
        row = lax.broadcasted_iota(jnp.int32, (L, 1), 0)
        sgn = (1 - 2 * (row & 1)).astype(F32)
        z_nyq = jnp.sum(z * sgn, axis=0, keepdims=True)
        y = acc_ref[...] + sgn * (z_nyq * nyq_ref[...] * inv_n) + bias_ref[...] * z

        @pl.when(o == 0)
        def _():
            z_ref[...] = x1_ref[...] * y

        @pl.when(o == 1)
        def _():
            o_ref[...] = (x2_ref[...] * y).astype(BF16)


def _hy_conv_call(hu, fre, fim, nyq, bias, cz, sz, L, kb, tc, base, n_seq):
    nct = W_HY // tc
    nkb = L // kb

    def zin(part):
        return pl.BlockSpec((L, tc), lambda b, c, o, k: (base + b, part * nct + c))

    fspec = pl.BlockSpec((kb, tc), lambda b, c, o, k: (k, o * nct + c))
    vspec = pl.BlockSpec((1, tc), lambda b, c, o, k: (0, o * nct + c))
    return pl.pallas_call(
        functools.partial(_hy_conv_body, L=L, kb=kb, nkb=nkb),
        grid=(n_seq, nct, HY_ORDER, nkb),
        in_specs=[zin(0), zin(1), zin(2), fspec, fspec, vspec, vspec,
                  pl.BlockSpec((kb, L), lambda b, c, o, k: (k, 0)), pl.BlockSpec((kb, L), lambda b, c, o, k: (k, 0)),
                  pl.BlockSpec((L, kb), lambda b, c, o, k: (0, k)), pl.BlockSpec((L, kb), lambda b, c, o, k: (0, k))],
        out_specs=pl.BlockSpec((L, tc), lambda b, c, o, k: (b, c)),
        out_shape=jax.ShapeDtypeStruct((n_seq * L, W_HY), BF16),
        scratch_shapes=[pltpu.VMEM((L, tc), F32), pltpu.VMEM((L, tc), BF16), pltpu.VMEM((L, tc), F32)],
        compiler_params=_cparams(("parallel", "parallel", "arbitrary", "arbitrary")),
        name="hy_conv",
    )(hu, hu, hu, fre, fim, nyq, bias, cz, sz, cz, sz)


def _hyena_tables(L):
    t = np.arange(L, dtype=np.float32) / np.float32(L)

    def emb(tt):
        ang = (np.float32(2.0 * math.pi) * tt[:, None] * np.arange(1, HY_BANDS + 1, dtype=np.float32)[None, :]).astype(np.float32)
        e = np.concatenate([tt[:, None], np.cos(ang), np.sin(ang)], axis=-1).astype(np.float32)
        return np.pad(e, ((0, 0), (0, 128 - HY_EMB)))

    tb = np.concatenate([t[:1], t[:0:-1]])
    kk = np.arange(L, dtype=np.int64)
    ang = (2.0 * math.pi / (2 * L)) * ((kk[:, None] * kk[None, :]) % (2 * L)).astype(np.float64)
    return dict(emb_f=jnp.asarray(emb(t), BF16), emb_b=jnp.asarray(emb(tb), BF16),
                t_f=jnp.asarray(t[:, None]), t_b=jnp.asarray(tb[:, None]),
                cz=jnp.asarray(np.cos(ang), BF16), sz=jnp.asarray(np.sin(ang), BF16))


def _rope_tables(L):
    nf = ML_DH // 4
    freqs = (np.float32(ROPE_BASE) ** (-np.arange(nf, dtype=np.float32) / np.float32(nf))).astype(np.float32)
    pos = np.arange(L)
    out_c, out_s = [], []
    for p in (pos // GRID_W, pos % GRID_W):
        ang = p.astype(np.float32)[:, None] * freqs[None, :]
        c, s = np.cos(ang).astype(np.float32), np.sin(ang).astype(np.float32)
        out_c += [c, c]
        out_s += [-s, s]
    return jnp.asarray(np.concatenate(out_c, axis=-1)), jnp.asarray(np.concatenate(out_s, axis=-1))


def kernel(x_prompt, x_sample, state_rglru, state_ssd, state_mlstm_C, state_mlstm_n, state_mlstm_m, c, c_ctx,
           ada_w, ada_b, ln_g, ln_b, ffn_wg, ffn_wu, ffn_wd, w_in, rg_conv_w, rg_conv_b, rg_gate_w, rg_gate_b,
           rg_lambda, hy_conv_w, hy_conv_b, hy_w1, hy_b1, hy_w2, hy_b2, hy_w3, hy_decay, hy_bias,
           ssd_conv_w, ssd_conv_b, ssd_dt_bias, ssd_A_log, ssd_D, ssd_norm_w, ml_gate_b, ml_norm_w,
           branch_w, mix_out):
    bp, lp_len = x_prompt.shape[:2]
    bs, ls = x_sample.shape[:2]
    mp, ms = bp * lp_len, bs * ls
    ncp, ncs = lp_len // CHUNK, ls // CHUNK

    def midx(i, bm):
        return jnp.where(i < mp // bm, 0, 1 + (i - mp // bm) // (ls // bm))

    cond = jnp.concatenate([c_ctx[None], c, jnp.zeros((8 - 1 - bs, D_MODEL), F32)], axis=0)
    s = jax.nn.silu(cond).astype(BF16)
    mod = _ada_call(s, ada_w, ada_b.reshape(DEPTH, 1, N_MOD * D_MODEL))
    mod = mod[:, :1 + bs].reshape(DEPTH, 1 + bs, N_MOD, 1, D_MODEL)

    def mod_row(l, j):
        return mod[l, :, j]

    o = IN_OFFSETS

    def wseg(a, b):
        return w_in[:, :, a:b]

    w_small = jnp.concatenate(
        [wseg(o[1], o[2]), wseg(0, o[1]), wseg(o[2], o[4]),
         wseg(o[4], o[5]), wseg(o[6], o[7]),
         jnp.zeros((DEPTH, D_MODEL, N_PROJ - O_SMALL - 2 * SSD_H - 4 * ML_NH), F32)], axis=-1).astype(BF16)
    w_qkvo = _shift_cast_call(w_in, o[5], 4 * W_ML)
    w_merge = _shift_cast_call(w_in, o[7], N_BRANCH * D_MODEL)
    wg00, wu00 = ffn_wg[0, 0].astype(BF16), ffn_wu[0, 0].astype(BF16)
    bw2d = branch_w.reshape(DEPTH, N_BRANCH * W_BRANCH, D_MODEL)
    gw_b = rg_gate_w.astype(BF16)
    hw1 = jnp.pad(hy_w1, ((0, 0), (0, 128 - HY_EMB), (0, 0))).astype(BF16)
    hw2, hw3 = hy_w2.astype(BF16), hy_w3.astype(BF16)

    tabs_p, tabs_s = _hyena_tables(lp_len), _hyena_tables(ls)
    rope_s = _rope_tables(ls)

    x = jnp.concatenate([x_prompt.reshape(mp, D_MODEL), x_sample.reshape(ms, D_MODEL)], axis=0)
    h = _modulate_call(x, mod_row(0, 0), mod_row(0, 1), midx)

    new_states = []
    wg1, wu1 = wg00, wu00
    for l in range(DEPTH):
        lng = ln_g[l].reshape(3, 1, D_MODEL)
        lnb = ln_b[l].reshape(3, 1, D_MODEL)

        a, (wd1,) = _ffn_up_call(h, wg1, wu1, (), sides=[(ffn_wd, (l, 0))])
        x, h = _res_ln_call(a, wd1, (), x, mod_row(l, 2), lng[0], lnb[0], mod_row(l, 3), mod_row(l, 4),
                            midx, 0.5, True)

        proj, (bw_l, mo_l) = _in_proj_call(h, w_small, (l,), sides=[(bw2d, (l,)), (mix_out, (l,))])
        proj_q, _ = _in_proj_call(h, w_qkvo, (l,))
        hu, xa, xbc = _conv_call(proj, hy_conv_w[l], hy_conv_b[l][None], rg_conv_w[l], rg_conv_b[l][None],
                                 ssd_conv_w[l], ssd_conv_b[l][None], mp // CHUNK, ncp, ncs)

        gb = rg_gate_b[l].reshape(4, W_RG)
        a_p, rg_last = _rglru_call(xa, proj, gw_b[l], gb, rg_lambda[l], None, 0, bp, ncp)
        a_s, _ = _rglru_call(xa, proj, gw_b[l], gb, rg_lambda[l], state_rglru[:, l], mp // CHUNK, bs, ncs)

        outs_b = []
        for tabs, L, kb, tc, base, n_seq in ((tabs_p, lp_len, lp_len, W_HY, 0, bp), (tabs_s, ls, 256, 256, mp // ls, bs)):
            fa, fb, nyq = _hy_filt_call(L, tabs['emb_f'], tabs['emb_b'], tabs['t_f'], tabs['t_b'], hw1[l], hy_b1[l][None],
                                        hw2[l], hy_b2[l][None], hw3[l], hy_decay[l][None])
            fre, fim = _hy_spec_call(L, kb, fa, fb, tabs['cz'], tabs['sz'])
            outs_b.append(_hy_conv_call(hu, fre, fim, nyq, hy_bias[l].reshape(1, HY_ORDER * W_HY), tabs['cz'], tabs['sz'],
                                        L, kb, tc, base, n_seq))

        dvec = jnp.repeat(ssd_D[l], SSD_P)[None]
        nw = ssd_norm_w[l][None]
        c_p, ssd_last = _ssd_call(xbc, proj, ssd_dt_bias[l], ssd_A_log[l], dvec, nw, None, 0, bp, ncp)
        c_s, _ = _ssd_call(xbc, proj, ssd_dt_bias[l], ssd_A_log[l], dvec, nw, state_ssd[:, l], mp // CHUNK, bs, ncs)

        mnw = ml_norm_w[l][None]
        d_p, cl, nl, ml = _mlstm_call(proj, proj_q, ml_gate_b[l], mnw, None, None, 0, bp, ncp)
        d_s, _, _, _ = _mlstm_call(proj, proj_q, ml_gate_b[l], mnw, rope_s,
                                   (state_mlstm_C[:, l], state_mlstm_n[:, l], state_mlstm_m[:, l]), mp // CHUNK, bs, ncs)
        new_states.append((rg_last, ssd_last, cl, nl, ml))

        merged, (wg2, wu2, wd2) = _merge_call(h, (a_p, outs_b[0], c_p, d_p), (a_s, outs_b[1], c_s, d_s), w_merge, (l,), bw_l,
                                              sides=[(ffn_wg, (l, 1)), (ffn_wu, (l, 1)), (ffn_wd, (l, 1))])
        x, h = _res_ln_call(merged, mo_l, (), x, mod_row(l, 5), lng[1], lnb[1], mod_row(l, 6), mod_row(l, 7),
                            midx, 1.0, True)

        last = l == DEPTH - 1
        a, cast = _ffn_up_call(h, wg2, wu2, (), sides=[] if last else [(ffn_wg, (l + 1, 0)), (ffn_wu, (l + 1, 0))])
        if not last:
            wg1, wu1 = cast
        nl_ = l if last else l + 1
        x, h = _res_ln_call(a, wd2, (), x, mod_row(l, 8), lng[2], lnb[2], mod_row(nl_, 0), mod_row(nl_, 1),
                            midx, 0.5, not last)

    yp = x[:mp].reshape(bp, lp_len, D_MODEL)
    ys = x[mp:].reshape(bs, ls, D_MODEL)
    outs = tuple(jnp.stack([new_states[l][j] for l in range(DEPTH)], axis=1) for j in range(5))
    return (yp, ys) + outs
```

```python
import functools
import math

import jax
import jax.numpy as jnp
import numpy as np
from jax import lax
from jax.experimental import pallas as pl
from jax.experimental.pallas import tpu as pltpu

D_MODEL = 4096
DEPTH = 2
GRID_W = 64
N_BRANCH = 4
W_BRANCH = D_MODEL // N_BRANCH
D_FF = 2 * D_MODEL
N_MOD = 9
CHUNK = 128
W_RG = W_BRANCH
RG_BLOCKS = 8
RG_BW = W_RG // RG_BLOCKS
RG_C = 8.0
W_HY = W_BRANCH
HY_ORDER = 2
HY_BANDS = 16
HY_EMB = 2 * HY_BANDS + 1
HY_FH = 64
SSD_DI = W_BRANCH
SSD_P = 64
SSD_H = SSD_DI // SSD_P
SSD_N = 128
SSD_G = 2
SSD_CONV_CH = SSD_DI + 2 * SSD_G * SSD_N
W_ML = W_BRANCH
ML_NH = 4
ML_DH = W_ML // ML_NH
ROPE_BASE = 10000.0
ALPHA = (2 * DEPTH) ** 0.25

IN_SIZES = (W_RG, W_RG, 3 * W_HY, SSD_DI, SSD_CONV_CH, 2 * SSD_H, 4 * W_ML, 4 * ML_NH, N_BRANCH * D_MODEL)
IN_OFFSETS = tuple(int(s) for s in np.cumsum(IN_SIZES)[:-1])

F32 = jnp.float32
BF16 = jnp.bfloat16
HI = lax.Precision.HIGHEST
NT = (((1,), (1,)), ((), ()))
TN = (((0,), (0,)), ((), ()))

O_HY = 0
O_RGX = O_HY + 3 * W_HY
O_RGY = O_RGX + W_RG
O_SSDZ = O_RGY + W_RG
O_XBC = O_SSDZ + SSD_DI
O_SMALL = O_XBC + SSD_CONV_CH
W_SMALL = 128
N_PROJ = 8192

VMEM_LIMIT = 56 * 1024 * 1024


def _cparams(sem):
    return pltpu.CompilerParams(dimension_semantics=sem, vmem_limit_bytes=VMEM_LIMIT)


def _dot(a, b):
    return jnp.dot(a, b, preferred_element_type=F32)


def _dg(a, b, dims, precision=None):
    return lax.dot_general(a, b, dims, preferred_element_type=F32, precision=precision)


def _ada_body(s_ref, w_ref, b_ref, o_ref):
    o_ref[...] = _dot(s_ref[...], w_ref[...].astype(BF16)) + b_ref[...]


def _ada_call(s, ada_w, ada_b):
    bn = 512
    n = ada_w.shape[-1]
    return pl.pallas_call(
        _ada_body,
        grid=(DEPTH, n // bn),
        in_specs=[pl.BlockSpec((8, D_MODEL), lambda l, j: (0, 0)),
                  pl.BlockSpec((None, D_MODEL, bn), lambda l, j: (l, 0, j)),
                  pl.BlockSpec((None, 1, bn), lambda l, j: (l, 0, j))],
        out_specs=pl.BlockSpec((None, 8, bn), lambda l, j: (l, 0, j)),
        out_shape=jax.ShapeDtypeStruct((DEPTH, 8, n), F32),
        compiler_params=_cparams(("parallel", "parallel")),
        name="ada_mod",
    )(s, ada_w, ada_b)


def _modulate_body(x_ref, sh_ref, sc_ref, o_ref):
    o_ref[...] = (x_ref[...] * (1.0 + sc_ref[...]) + sh_ref[...]).astype(BF16)


def _modulate_call(x, shift, scale, midx):
    m = x.shape[0]
    bm = 256
    mspec = pl.BlockSpec((None, 1, D_MODEL), lambda i: (midx(i, bm), 0, 0))
    return pl.pallas_call(
        _modulate_body,
        grid=(m // bm,),
        in_specs=[pl.BlockSpec((bm, D_MODEL), lambda i: (i, 0)), mspec, mspec],
        out_specs=pl.BlockSpec((bm, D_MODEL), lambda i: (i, 0)),
        out_shape=jax.ShapeDtypeStruct((m, D_MODEL), BF16),
        compiler_params=_cparams(("parallel",)),
        name="modulate",
    )(x, shift, scale)


def _ffn_up_body(h_ref, wg_ref, wu_ref, o_ref):
    h = h_ref[...]
    g = _dot(h, wg_ref[...])
    u = _dot(h, wu_ref[...])
    o_ref[...] = (g * jax.nn.sigmoid(g) * u).astype(BF16)


def _wspec(lead, block, index_fn):
    lead = tuple(lead)
    return pl.BlockSpec((None,) * len(lead) + tuple(block), lambda *g: lead + tuple(index_fn(*g)))


def _side_plan(sides, grid):
    n_steps = grid[0] * grid[1]
    ins, outs, shapes = [], [], []
    for src, lead in sides:
        r, c = src.shape[-2:]
        rb = next(b for b in (16, 32, 64, 128, 256, 512) if r % b == 0 and r // b <= n_steps)
        nblk = r // rb

        def blk(i, j, nblk=nblk):
            return jnp.minimum(i * grid[1] + j, nblk - 1)

        ins.append(_wspec(lead, (rb, c), lambda i, j, blk=blk: (blk(i, j), 0)))
        outs.append(pl.BlockSpec((rb, c), lambda i, j, blk=blk: (blk(i, j), 0)))
        shapes.append(jax.ShapeDtypeStruct((r, c), BF16))
    return ins, outs, shapes


def _with_sides(body, n_in, n_out, n_side):
    def wrapped(*refs):
        ins, rest = refs[:n_in], refs[n_in:]
        side_src, rest = rest[:n_side], rest[n_side:]
        outs, rest = rest[:n_out], rest[n_out:]
        side_dst, scratch = rest[:n_side], rest[n_side:]
        body(*ins, *outs, *scratch)
        for s, d in zip(side_src, side_dst):
            d[...] = s[...].astype(BF16)
    return wrapped


def _ffn_up_call(h, wg, wu, lead, sides=()):
    m = h.shape[0]
    bm, bn = 1024, 512
    grid = (m // bm, D_FF // bn)
    s_in, s_out, s_shape = _side_plan(sides, grid)
    res = pl.pallas_call(
        _with_sides(_ffn_up_body, 3, 1, len(sides)),
        grid=grid,
        in_specs=[pl.BlockSpec((bm, D_MODEL), lambda i, j: (i, 0)),
                  _wspec(lead, (D_MODEL, bn), lambda i, j: (0, j)),
                  _wspec(lead, (D_MODEL, bn), lambda i, j: (0, j))] + s_in,
        out_specs=[pl.BlockSpec((bm, bn), lambda i, j: (i, j))] + s_out,
        out_shape=[jax.ShapeDtypeStruct((m, D_FF), BF16)] + s_shape,
        compiler_params=_cparams(("arbitrary", "arbitrary")),
        name="ffn_up",
    )(h, wg, wu, *[s for s, _ in sides])
    return res[0], res[1:]


def _in_proj_body(h_ref, wt_ref, o_ref):
    o_ref[...] = _dg(h_ref[...], wt_ref[...], NT)


def _in_proj_call(h, wt, lead, sides=()):
    m = h.shape[0]
    n = wt.shape[-2]
    bm, bn = 1024, 512
    grid = (m // bm, n // bn)
    s_in, s_out, s_shape = _side_plan(sides, grid)
    res = pl.pallas_call(
        _with_sides(_in_proj_body, 2, 1, len(sides)),
        grid=grid,
        in_specs=[pl.BlockSpec((bm, D_MODEL), lambda i, j: (i, 0)),
                  _wspec(lead, (bn, D_MODEL), lambda i, j: (j, 0))] + s_in,
        out_specs=[pl.BlockSpec((bm, bn), lambda i, j: (i, j))] + s_out,
        out_shape=[jax.ShapeDtypeStruct((m, n), F32)] + s_shape,
        compiler_params=_cparams(("arbitrary", "arbitrary")),
        name="in_proj",
    )(h, wt, *[s for s, _ in sides])
    return res[0], res[1:]


def _merge_body(*refs, p_blocks):
    h_ref, bp, bs, wg, wb, o_ref = refs[0], refs[1:5], refs[5:9], refs[9:13], refs[13:17], refs[17]
    h = h_ref[...]
    is_prompt = pl.program_id(0) < p_blocks
    acc = None
    for j in range(N_BRANCH):
        br = jnp.where(is_prompt, bp[j][...], bs[j][...])
        t = jax.nn.sigmoid(_dg(h, wg[j][...], NT)) * _dot(br, wb[j][...])
        acc = t if acc is None else acc + t
    o_ref[...] = acc.astype(BF16)


def _merge_call(h, br_prompt, br_sample, w_merge, lead, branch_w, sides=()):
    m = h.shape[0]
    bm, bn = 512, 256
    nb = D_MODEL // bn
    grid = (m // bm, nb)
    p_blocks = br_prompt[0].shape[0] // bm
    bp_specs = [pl.BlockSpec((bm, W_BRANCH), lambda i, n: (jnp.minimum(i, p_blocks - 1), 0)) for _ in range(N_BRANCH)]
    bs_specs = [pl.BlockSpec((bm, W_BRANCH), lambda i, n: (jnp.maximum(i - p_blocks, 0), 0)) for _ in range(N_BRANCH)]
    wg_specs = [_wspec(lead, (bn, D_MODEL), functools.partial(lambda i, n, j: (j * nb + n, 0), j=j))
                for j in range(N_BRANCH)]
    wb_specs = [pl.BlockSpec((W_BRANCH, bn), functools.partial(lambda i, n, j: (j, n), j=j)) for j in range(N_BRANCH)]
    s_in, s_out, s_shape = _side_plan(sides, grid)
    res = pl.pallas_call(
        _with_sides(functools.partial(_merge_body, p_blocks=p_blocks), 1 + 4 * N_BRANCH, 1, len(sides)),
        grid=grid,
        in_specs=[pl.BlockSpec((bm, D_MODEL), lambda i, n: (i, 0))] + bp_specs + bs_specs + wg_specs + wb_specs + s_in,
        out_specs=[pl.BlockSpec((bm, bn), lambda i, n: (i, n))] + s_out,
        out_shape=[jax.ShapeDtypeStruct((m, D_MODEL), BF16)] + s_shape,
        compiler_params=_cparams(("arbitrary", "arbitrary")),
        name="merge",
    )(h, *br_prompt, *br_sample, *([w_merge] * N_BRANCH), *([branch_w] * N_BRANCH), *[s for s, _ in sides])
    return res[0], res[1:]


def _res_ln_body(a_ref, w_ref, x_ref, gate_ref, lng_ref, lnb_ref, nsh_ref, nsc_ref, y_ref, hn_ref, st_ref, *,
                 coef, nk, bm, emit_next):
    k = pl.program_id(1)

    @pl.when(k == 0)
    def _():
        y_ref[...] = _dot(a_ref[...], w_ref[...])

    @pl.when(k > 0)
    def _():
        y_ref[...] += _dot(a_ref[...], w_ref[...])

    @pl.when(k == nk - 1)
    def _():
        rows = 8
        gate = coef * gate_ref[...]
        lng, lnb = lng_ref[...], lnb_ref[...]
        nsc, nsh = 1.0 + nsc_ref[...], nsh_ref[...]

        def rows_of(c):
            return pl.ds(pl.multiple_of(c * rows, rows), rows)

        def sweep_mean(c, carry):
            rs = rows_of(c)
            r = ALPHA * x_ref[rs, :] + gate * y_ref[rs, :]
            y_ref[rs, :] = r
            st_ref[rs, 0:1] = jnp.mean(r, axis=-1, keepdims=True)
            return carry

        def sweep_var(c, carry):
            rs = rows_of(c)
            d = y_ref[rs, :] - st_ref[rs, 0:1]
            st_ref[rs, 1:2] = lax.rsqrt(jnp.mean(d * d, axis=-1, keepdims=True) + 1e-5)
            return carry

        def sweep_out(c, carry):
            rs = rows_of(c)
            o = (y_ref[rs, :] - st_ref[rs, 0:1]) * st_ref[rs, 1:2] * lng + lnb
            y_ref[rs, :] = o
            if emit_next:
                hn_ref[rs, :] = (o * nsc + nsh).astype(BF16)
            return carry

        for sweep, unroll in ((sweep_mean, 8), (sweep_var, 8), (sweep_out, 4)):
            lax.fori_loop(0, bm // rows, sweep, 0, unroll=unroll)
        if not emit_next:
            hn_ref[...] = jnp.zeros_like(hn_ref)


def _res_ln_call(a, w, lead, x, gate, lng, lnb, nshift, nscale, midx, coef, emit_next):
    m, kdim = a.shape
    bm, bk = 512, 512
    nk = kdim // bk
    hn_rows = bm if emit_next else 8
    mspec = pl.BlockSpec((None, 1, D_MODEL), lambda i, k: (midx(i, bm), 0, 0))
    vspec = pl.BlockSpec((1, D_MODEL), lambda i, k: (0, 0))
    y, hn = pl.pallas_call(
        functools.partial(_res_ln_body, coef=coef, nk=nk, bm=bm, emit_next=emit_next),
        grid=(m // bm, nk),
        in_specs=[pl.BlockSpec((bm, bk), lambda i, k: (i, k)),
                  _wspec(lead, (bk, D_MODEL), lambda i, k: (k, 0)),
                  pl.BlockSpec((bm, D_MODEL), lambda i, k: (i, 0)),
                  mspec, vspec, vspec, mspec, mspec],
        out_specs=[pl.BlockSpec((bm, D_MODEL), lambda i, k: (i, 0)),
                   pl.BlockSpec((hn_rows, D_MODEL), lambda i, k: (i, 0))],
        out_shape=[jax.ShapeDtypeStruct((m, D_MODEL), F32),
                   jax.ShapeDtypeStruct((m if emit_next else 8 * (m // bm), D_MODEL), BF16)],
        scratch_shapes=[pltpu.VMEM((bm, 128), F32)],
        compiler_params=_cparams(("parallel", "arbitrary")),
        name="res_ln",
    )(a, w, x, gate, lng, lnb, nshift, nscale)
    return y, hn


def _conv_piece(pad_ref, w_ref, b_ref, o_ref, c0, width, taps, pad_l, act):
    cs = slice(c0, c0 + width)
    acc = b_ref[:, cs]
    for j in range(taps):
        r0 = 8 - pad_l + j
        acc = acc + w_ref[j:j + 1, cs] * pad_ref[r0:r0 + CHUNK, cs]
    o_ref[:, cs] = act(acc)


def _conv_body(hy_p, hy_c, hy_n, rg_p, rg_c, rg_n, xb_p, xb_c, xb_n,
               hw_ref, hb_ref, rw_ref, rb_ref, xw_ref, xb_ref,
               hu_ref, xa_ref, xbc_ref, pad_ref, *, chunk_pos):
    first, last = chunk_pos(pl.program_id(0))
    for prev, cur, nxt, w_ref, b_ref, o_ref, width, taps, pad_l, act in (
            (hy_p, hy_c, hy_n, hw_ref, hb_ref, hu_ref, 3 * W_HY, 3, 1, lambda v: v),
            (rg_p, rg_c, rg_n, rw_ref, rb_ref, xa_ref, W_RG, 4, 2, lambda v: v),
            (xb_p, xb_c, xb_n, xw_ref, xb_ref, xbc_ref, SSD_CONV_CH, 4, 2, jax.nn.silu)):
        pad_ref[0:8, 0:width] = jnp.where(first, 0.0, prev[...])
        pad_ref[8:8 + CHUNK, 0:width] = cur[...]
        pad_ref[8 + CHUNK:16 + CHUNK, 0:width] = jnp.where(last, 0.0, nxt[...])
        for c0 in range(0, width, 512):
            _conv_piece(pad_ref, w_ref, b_ref, o_ref, c0, min(512, width - c0), taps, pad_l, act)


def _conv_call(proj, hy_w, hy_b, rg_w, rg_b, xb_w, xb_b, mp_chunks, ncp, ncs):
    m = proj.shape[0]
    n8 = m // 8

    def chunk_pos(i):
        pos = jnp.where(i < mp_chunks, i % ncp, (i - mp_chunks) % ncs)
        n = jnp.where(i < mp_chunks, ncp, ncs)
        return pos == 0, pos == n - 1

    def trio(width, col):
        return [pl.BlockSpec((8, width), lambda i: (jnp.maximum(i * (CHUNK // 8) - 1, 0), col)),
                pl.BlockSpec((CHUNK, width), lambda i: (i, col)),
                pl.BlockSpec((8, width), lambda i: (jnp.minimum((i + 1) * (CHUNK // 8), n8 - 1), col))]

    def full(a):
        return pl.BlockSpec(a.shape, lambda i: (0, 0))

    params = (hy_w, hy_b, rg_w, rg_b, xb_w, xb_b)
    return pl.pallas_call(
        functools.partial(_conv_body, chunk_pos=chunk_pos),
        grid=(m // CHUNK,),
        in_specs=trio(3 * W_HY, O_HY // (3 * W_HY)) + trio(W_RG, O_RGX // W_RG)
        + trio(SSD_CONV_CH, O_XBC // SSD_CONV_CH) + [full(a) for a in params],
        out_specs=[pl.BlockSpec((CHUNK, 3 * W_HY), lambda i: (i, 0)),
                   pl.BlockSpec((CHUNK, W_RG), lambda i: (i, 0)),
                   pl.BlockSpec((CHUNK, SSD_CONV_CH), lambda i: (i, 0))],
        out_shape=[jax.ShapeDtypeStruct((m, 3 * W_HY), F32),
                   jax.ShapeDtypeStruct((m, W_RG), F32),
                   jax.ShapeDtypeStruct((m, SSD_CONV_CH), F32)],
        scratch_shapes=[pltpu.VMEM((CHUNK + 16, 3 * W_HY), F32)],
        compiler_params=_cparams(("parallel",)),
        name="dw_conv",
    )(*([proj] * 9), *params)


def _chunk_of(j, nc):
    return jnp.where(j < nc, j, 2 * nc - 1 - j)


def _row_map(base, nc, col):
    return lambda b, j: (base + b * nc + _chunk_of(j, nc), col)


def _out_map(nc):
    return lambda b, j: (b * nc + jnp.where(j < nc, nc - 1, 2 * nc - 1 - j), 0)


def _tri(rev):
    r = lax.broadcasted_iota(jnp.int32, (CHUNK, CHUNK), 0)
    c = lax.broadcasted_iota(jnp.int32, (CHUNK, CHUNK), 1)
    return (c >= r) if rev else (c <= r)


def _two_phase(nc, run):
    j = pl.program_id(1)

    @pl.when(j < nc)
    def _():
        run(0, j == 0, j == nc - 1, j)

    @pl.when(j >= nc)
    def _():
        run(1, j == nc, j == 2 * nc - 1, 2 * nc - 1 - j)


def _rglru_body(xa_ref, y_ref, gw_ref, gb_ref, lam_ref, h0_ref, o_ref, hl_ref,
                yf_ref, a_ref, u_ref, hb_ref, hc_ref, *, nc, init):
    def run(d, is_first, is_last, ci):
        x = xa_ref[...]
        xb = x.astype(BF16)
        sp = jax.nn.softplus(-lam_ref[d:d + 1, :])
        for n in range(RG_BLOCKS):
            cs = slice(n * RG_BW, (n + 1) * RG_BW)
            r = jax.nn.sigmoid(_dot(xb[:, cs], gw_ref[d, 0, n]) + gb_ref[2 * d:2 * d + 1, cs])
            i = jax.nn.sigmoid(_dot(xb[:, cs], gw_ref[d, 1, n]) + gb_ref[2 * d + 1:2 * d + 2, cs])
            log_a = -RG_C * r * sp[:, cs]
            th = jnp.tanh(log_a)
            a_ref[:, cs] = jnp.exp(log_a)
            u_ref[:, cs] = jnp.sqrt(-2.0 * th / (1.0 - th)) * (i * x[:, cs])

        @pl.when(is_first)
        def _():
            hc_ref[0:1, :] = h0_ref[d:d + 1, :] if init else jnp.zeros((1, W_RG), F32)

        def step(s, h):
            t = (CHUNK - 1 - s) if d else s
            h = a_ref[pl.ds(t, 1), :] * h + u_ref[pl.ds(t, 1), :]
            hb_ref[pl.ds(t, 1), :] = h
            return h

        h = lax.fori_loop(0, CHUNK, step, hc_ref[0:1, :], unroll=8)
        hc_ref[0:1, :] = h

        @pl.when(is_last)
        def _():
            hl_ref[d:d + 1, :] = h

        r0 = pl.multiple_of(ci * CHUNK, CHUNK)
        if d == 0:
            yf_ref[pl.ds(r0, CHUNK), :] = hb_ref[...]
        else:
            o_ref[...] = ((yf_ref[pl.ds(r0, CHUNK), :] + hb_ref[...]) * jax.nn.gelu(y_ref[...])).astype(BF16)

    _two_phase(nc, run)


def _rglru_call(xa, proj, gw, gb, lam, h0, base, n_seq, nc):
    init = h0 is not None
    if not init:
        h0 = jnp.zeros((n_seq, 2, W_RG), F32)
    L = nc * CHUNK
    st_spec = pl.BlockSpec((None, 2, W_RG), lambda b, j: (b, 0, 0))
    return pl.pallas_call(
        functools.partial(_rglru_body, nc=nc, init=init),
        grid=(n_seq, 2 * nc),
        in_specs=[pl.BlockSpec((CHUNK, W_RG), _row_map(base, nc, 0)),
                  pl.BlockSpec((CHUNK, W_RG), _row_map(base, nc, O_RGY // W_RG)),
                  pl.BlockSpec(gw.shape, lambda b, j: (0, 0, 0, 0, 0)),
                  pl.BlockSpec(gb.shape, lambda b, j: (0, 0)),
                  pl.BlockSpec(lam.shape, lambda b, j: (0, 0)),
                  st_spec],
        out_specs=[pl.BlockSpec((CHUNK, W_RG), _out_map(nc)), st_spec],
        out_shape=[jax.ShapeDtypeStruct((n_seq * L, W_RG), BF16),
                   jax.ShapeDtypeStruct((n_seq, 2, W_RG), F32)],
        scratch_shapes=[pltpu.VMEM((L, W_RG), F32), pltpu.VMEM((CHUNK, W_RG), F32), pltpu.VMEM((CHUNK, W_RG), F32),
                        pltpu.VMEM((CHUNK, W_RG), F32), pltpu.VMEM((8, W_RG), F32)],
        compiler_params=_cparams(("parallel", "arbitrary")),
        name="rglru",
    )(xa, proj, gw, gb, lam, h0)


def _ssd_body(xbc_ref, z_ref, sm_ref, dtb_r, dtb_c, al_r, al_c, dvec_ref, nw_ref, s0_ref, o_ref, sl_ref,
              yf_ref, yb_ref, s_ref, *, nc, init):
    def run(d, is_first, is_last, ci):
        hs = slice(SSD_H * d, SSD_H * (d + 1))
        sm = sm_ref[...]
        smt = sm.T
        dt_c = jax.nn.softplus(sm[:, hs] + dtb_r[:, hs])
        dt_r = jax.nn.softplus(smt[hs, :] + dtb_c[hs, :])
        mask = _tri(d)
        tri = jnp.where(mask, 1.0, 0.0).astype(F32)
        cs_c = jnp.dot(tri, dt_c * -jnp.exp(al_r[:, hs]), preferred_element_type=F32, precision=HI)
        cs_r = _dg(dt_r * -jnp.exp(al_c[hs, :]), tri, NT, HI)
        e = 0 if d else CHUNK - 1
        tot = cs_c[e:e + 1, :]
        w_end = jnp.exp(tot - cs_c) * dt_c
        ecs = jnp.exp(cs_c)
        etot = jnp.exp(tot)

        @pl.when(is_first)
        def _():
            s_ref[...] = s0_ref[d] if init else jnp.zeros(s_ref.shape, F32)

        xs = xbc_ref[:, 0:SSD_DI]
        xsb = xs.astype(BF16)
        for g in range(SSD_G):
            bb = xbc_ref[:, SSD_DI + g * SSD_N:SSD_DI + (g + 1) * SSD_N].astype(BF16)
            cb = xbc_ref[:, SSD_DI + (SSD_G + g) * SSD_N:SSD_DI + (SSD_G + g + 1) * SSD_N].astype(BF16)
            gmat = _dg(cb, bb, NT)
            for hh in range(SSD_H // SSD_G):
                h = g * (SSD_H // SSD_G) + hh
                ps = slice(h * SSD_P, (h + 1) * SSD_P)
                seg = cs_c[:, h:h + 1] - cs_r[h:h + 1, :]
                dec = jnp.exp(jnp.where(mask, seg, -jnp.inf))
                mh = (gmat * dec * dt_r[h:h + 1, :]).astype(BF16)
                sh = s_ref[h]
                yb_ref[:, ps] = _dot(mh, xsb[:, ps]) + _dg(cb, sh.astype(BF16), NT) * ecs[:, h:h + 1]
                xw = (xs[:, ps] * w_end[:, h:h + 1]).astype(BF16)
                s_ref[h] = etot[:, h:h + 1] * sh + _dg(xw, bb, TN)

        @pl.when(is_last)
        def _():
            sl_ref[d] = s_ref[...]

        r0 = pl.multiple_of(ci * CHUNK, CHUNK)
        if d == 0:
            yf_ref[pl.ds(r0, CHUNK), :] = yb_ref[...]
        else:
            yc = yf_ref[pl.ds(r0, CHUNK), :] + yb_ref[...] + dvec_ref[...] * xs
            v = yc * jax.nn.silu(z_ref[...])
            o_ref[...] = (v * lax.rsqrt(jnp.mean(v * v, axis=-1, keepdims=True) + 1e-6) * nw_ref[...]).astype(BF16)

    _two_phase(nc, run)


def _ssd_call(xbc, proj, dt_bias, a_log, dvec, norm_w, s0, base, n_seq, nc):
    init = s0 is not None
    if not init:
        s0 = jnp.zeros((n_seq, 2, 8, 8, SSD_N), F32)
    L = nc * CHUNK
    st_shape = (SSD_H, SSD_P, SSD_N)
    dtb_r, al_r = dt_bias.reshape(1, 2 * SSD_H), a_log.reshape(1, 2 * SSD_H)
    dtb_c, al_c = dt_bias.reshape(2 * SSD_H, 1), a_log.reshape(2 * SSD_H, 1)

    def full(a):
        return pl.BlockSpec(a.shape, lambda b, j: (0,) * a.ndim)

    s0_spec = pl.BlockSpec((None,) + s0.shape[1:], lambda b, j: (b, 0, 0, 0, 0))
    return pl.pallas_call(
        functools.partial(_ssd_body, nc=nc, init=init),
        grid=(n_seq, 2 * nc),
        in_specs=[pl.BlockSpec((CHUNK, SSD_CONV_CH), _row_map(base, nc, 0)),
                  pl.BlockSpec((CHUNK, SSD_DI), _row_map(base, nc, O_SSDZ // SSD_DI)),
                  pl.BlockSpec((CHUNK, W_SMALL), _row_map(base, nc, O_SMALL // W_SMALL)),
                  full(dtb_r), full(dtb_c), full(al_r), full(al_c), full(dvec), full(norm_w), s0_spec],
        out_specs=[pl.BlockSpec((CHUNK, SSD_DI), _out_map(nc)),
                   pl.BlockSpec((None, 2) + st_shape, lambda b, j: (b, 0, 0, 0, 0))],
        out_shape=[jax.ShapeDtypeStruct((n_seq * L, SSD_DI), BF16),
                   jax.ShapeDtypeStruct((n_seq, 2) + st_shape, F32)],
        scratch_shapes=[pltpu.VMEM((L, SSD_DI), F32), pltpu.VMEM((CHUNK, SSD_DI), F32), pltpu.VMEM(st_shape, F32)],
        compiler_params=_cparams(("parallel", "arbitrary")),
        name="ssd",
    )(xbc, proj, proj, dtb_r, dtb_c, al_r, al_c, dvec, norm_w, s0)


def _mlstm_body(qkvo_ref, sm_ref, gb_r, gb_c, nw_ref, cos_ref, sin_ref, c0_ref, n0_ref, m0_ref,
                o_ref, cl_ref, nl_ref, ml_ref, hf_ref, hb_ref, c_ref, n_ref, m_ref, *, nc, init, rope):
    def run(d, is_first, is_last, ci):
        sm = sm_ref[...]
        smt = sm.T
        o_i, o_f = 2 * SSD_H + 8 * d, 2 * SSD_H + 8 * d + ML_NH
        i_c = sm[:, o_i:o_i + ML_NH] + gb_r[:, 8 * d:8 * d + ML_NH]
        i_r = smt[o_i:o_i + ML_NH, :] + gb_c[8 * d:8 * d + ML_NH, :]
        lf_c = jax.nn.log_sigmoid(sm[:, o_f:o_f + ML_NH] + gb_r[:, 8 * d + ML_NH:8 * d + 2 * ML_NH])
        lf_r = jax.nn.log_sigmoid(smt[o_f:o_f + ML_NH, :] + gb_c[8 * d + ML_NH:8 * d + 2 * ML_NH, :])
        mask = _tri(d)
        tri = jnp.where(mask, 1.0, 0.0).astype(F32)
        b_c = jnp.dot(tri, lf_c, preferred_element_type=F32, precision=HI)
        b_r = _dg(lf_r, tri, NT, HI)
        e = 0 if d else CHUNK - 1
        b_end = b_c[e:e + 1, :]

        @pl.when(is_first)
        def _():
            if init:
                c_ref[...] = c0_ref[d]
                n_ref[0:ML_NH, :] = n0_ref[d]
                m_ref[0:1, 0:ML_NH] = m0_ref[d:d + 1, :]
            else:
                c_ref[...] = jnp.zeros(c_ref.shape, F32)
                n_ref[...] = jnp.zeros(n_ref.shape, F32)
                m_ref[...] = jnp.zeros(m_ref.shape, F32)

        def rot(t):
            if not rope:
                return t
            parts = []
            for p in range(ML_DH // 128):
                tp = t[:, p * 128:(p + 1) * 128]
                parts.append(tp * cos_ref[:, p * 128:(p + 1) * 128]
                             + pltpu.roll(tp, 64, axis=1) * sin_ref[:, p * 128:(p + 1) * 128])
            return jnp.concatenate(parts, axis=-1)

        for h in range(ML_NH):
            hsl = slice(h * ML_DH, (h + 1) * ML_DH)
            q = rot(qkvo_ref[:, h * ML_DH:(h + 1) * ML_DH])
            k = rot(qkvo_ref[:, W_ML + h * ML_DH:W_ML + (h + 1) * ML_DH] * (ML_DH ** -0.5))
            v = qkvo_ref[:, 2 * W_ML + h * ML_DH:2 * W_ML + (h + 1) * ML_DH]
            qb, kb = q.astype(BF16), k.astype(BF16)
            m_prev = m_ref[0:1, h:h + 1]
            dmat = jnp.where(mask, b_c[:, h:h + 1] - b_r[h:h + 1, :] + i_r[h:h + 1, :], -jnp.inf)
            g0 = b_c[:, h:h + 1] + m_prev
            mt = jnp.maximum(g0, jnp.max(dmat, axis=1, keepdims=True))
            w = jnp.exp(dmat - mt)
            w0 = jnp.exp(g0 - mt)
            s = _dg(qb, kb, NT) * w
            ch = c_ref[h]
            nh = n_ref[h:h + 1, :]
            num = _dot(s.astype(BF16), v.astype(BF16)) + w0 * _dg(qb, ch.astype(BF16), NT)
            den = jnp.sum(s, axis=1, keepdims=True) + w0 * jnp.sum(q * nh, axis=1, keepdims=True)
            hb_ref[:, hsl] = num / jnp.maximum(jnp.abs(den), jnp.exp(-mt))
            d_end = b_end[:, h:h + 1] - b_c[:, h:h + 1] + i_c[:, h:h + 1]
            g0e = b_end[:, h:h + 1] + m_prev
            m_new = jnp.maximum(g0e, jnp.max(d_end, axis=0, keepdims=True))
            we = jnp.exp(d_end - m_new)
            w0e = jnp.exp(g0e - m_new)
            c_ref[h] = w0e * ch + _dg((v * we).astype(BF16), kb, TN)
            n_ref[h:h + 1, :] = w0e * nh + jnp.sum(we * k, axis=0, keepdims=True)
            m_ref[0:1, h:h + 1] = m_new

        @pl.when(is_last)
        def _():
            cl_ref[d] = c_ref[...]
            nl_ref[d] = n_ref[0:ML_NH, :]
            ml_ref[d:d + 1, :] = m_ref[0:1, 0:ML_NH]

        r0 = pl.multiple_of(ci * CHUNK, CHUNK)
        if d == 0:
            hf_ref[pl.ds(r0, CHUNK), :] = hb_ref[...]
        else:
            for h in range(ML_NH):
                hsl = slice(h * ML_DH, (h + 1) * ML_DH)
                t = hf_ref[pl.ds(r0, CHUNK), hsl] + hb_ref[:, hsl]
                mu = jnp.mean(t, axis=-1, keepdims=True)
                var = jnp.mean(jnp.square(t - mu), axis=-1, keepdims=True)
                og = jax.nn.sigmoid(qkvo_ref[:, 3 * W_ML + h * ML_DH:3 * W_ML + (h + 1) * ML_DH])
                o_ref[:, hsl] = (og * ((t - mu) * lax.rsqrt(var + 1e-5) * nw_ref[:, hsl])).astype(BF16)

    _two_phase(nc, run)


def _mlstm_call(proj, proj_q, gate_b, norm_w, rope_tabs, st0, base, n_seq, nc):
    init = st0 is not None
    rope = rope_tabs is not None
    L = nc * CHUNK
    if init:
        c0, n0, m0 = st0
    else:
        c0, n0, m0 = jnp.zeros((n_seq, 2, 1, 8, 128), F32), jnp.zeros((n_seq, 2, 8, 128), F32), jnp.zeros((n_seq, 2, ML_NH), F32)
    if rope:
        cos_t, sin_t = rope_tabs
        tab_spec = pl.BlockSpec((CHUNK, ML_DH), lambda b, j: (_chunk_of(j, nc), 0))
    else:
        cos_t = sin_t = jnp.zeros((8, 128), F32)
        tab_spec = pl.BlockSpec((8, 128), lambda b, j: (0, 0))
    gb_r, gb_c = gate_b.reshape(1, 4 * ML_NH), gate_b.reshape(4 * ML_NH, 1)

    def full(a):
        return pl.BlockSpec(a.shape, lambda b, j: (0,) * a.ndim)

    def per_seq(a):
        return pl.BlockSpec((None,) + a.shape[1:], lambda b, j: (b,) + (0,) * (a.ndim - 1))

    c_shape, n_shape = (ML_NH, ML_DH, ML_DH), (ML_NH, ML_DH)
    return pl.pallas_call(
        functools.partial(_mlstm_body, nc=nc, init=init, rope=rope),
        grid=(n_seq, 2 * nc),
        in_specs=[pl.BlockSpec((CHUNK, 4 * W_ML), _row_map(base, nc, 0)),
                  pl.BlockSpec((CHUNK, W_SMALL), _row_map(base, nc, O_SMALL // W_SMALL)),
                  full(gb_r), full(gb_c), full(norm_w), tab_spec, tab_spec,
                  per_seq(c0), per_seq(n0), per_seq(m0)],
        out_specs=[pl.BlockSpec((CHUNK, W_ML), _out_map(nc)),
                   pl.BlockSpec((None, 2) + c_shape, lambda b, j: (b, 0, 0, 0, 0)),
                   pl.BlockSpec((None, 2) + n_shape, lambda b, j: (b, 0, 0, 0)),
                   pl.BlockSpec((None, 2, ML_NH), lambda b, j: (b, 0, 0))],
        out_shape=[jax.ShapeDtypeStruct((n_seq * L, W_ML), BF16),
                   jax.ShapeDtypeStruct((n_seq, 2) + c_shape, F32),
                   jax.ShapeDtypeStruct((n_seq, 2) + n_shape, F32),
                   jax.ShapeDtypeStruct((n_seq, 2, ML_NH), F32)],
        scratch_shapes=[pltpu.VMEM((L, W_ML), F32), pltpu.VMEM((CHUNK, W_ML), F32), pltpu.VMEM(c_shape, F32),
                        pltpu.VMEM((8, ML_DH), F32), pltpu.VMEM((8, 128), F32)],
        compiler_params=_cparams(("parallel", "arbitrary")),
        name="mlstm",
    )(proj_q, proj, gb_r, gb_c, norm_w, cos_t, sin_t, c0, n0, m0)


def _hy_filt_body(ef_ref, eb_ref, tf_ref, tb_ref, w1_ref, b1_ref, w2_ref, b2_ref, w3f_ref, w3b_ref,
                  dcf_ref, dcb_ref, fa_ref, fb_ref, nyq_ref, hidf_ref, hidb_ref, *, L):
    @pl.when((pl.program_id(0) == 0) & (pl.program_id(1) == 0))
    def _():
        for e_ref, hid_ref in ((ef_ref, hidf_ref), (eb_ref, hidb_ref)):
            f = jnp.sin(_dot(e_ref[...], w1_ref[...]) + b1_ref[...])
            hid_ref[...] = jnp.sin(_dot(f.astype(BF16), w2_ref[...]) + b2_ref[...]).astype(BF16)

    def mlp(hid_ref, t_ref, w3_ref, dc_ref):
        return _dot(hid_ref[...], w3_ref[...]) * jnp.exp(-dc_ref[...] * t_ref[...])

    hf = mlp(hidf_ref, tf_ref, w3f_ref, dcf_ref)
    row = lax.broadcasted_iota(jnp.int32, hf.shape, 0)
    hb = jnp.where(row == 0, 0.0, mlp(hidb_ref, tb_ref, w3b_ref, dcb_ref))
    norm = jnp.sum(jnp.abs(hf), axis=0, keepdims=True) + jnp.sum(jnp.abs(hb), axis=0, keepdims=True) + 1e-6
    hf, hb = hf / norm, hb / norm
    sgn = (1 - 2 * (row & 1)).astype(F32)
    nyq_ref[...] = jnp.sum(sgn * (hf + hb), axis=0, keepdims=True)
    fa_ref[...] = hf.astype(BF16)
    fb_ref[...] = hb.astype(BF16)


def _hy_filt_call(L, emb_f, emb_b, t_f, t_b, w1, b1, w2, b2, w3, decay):
    tc = 256
    nct = W_HY // tc

    def full(a):
        return pl.BlockSpec(a.shape, lambda o, c: (0, 0))

    def col(dirn):
        return lambda o, c: (0, (2 * o + dirn) * nct + c)

    out_spec = pl.BlockSpec((L, tc), lambda o, c: (0, o * nct + c))
    return pl.pallas_call(
        functools.partial(_hy_filt_body, L=L),
        grid=(HY_ORDER, nct),
        in_specs=[full(emb_f), full(emb_b), full(t_f), full(t_b), full(w1), full(b1), full(w2), full(b2),
                  pl.BlockSpec((HY_FH, tc), col(0)), pl.BlockSpec((HY_FH, tc), col(1)),
                  pl.BlockSpec((1, tc), col(0)), pl.BlockSpec((1, tc), col(1))],
        out_specs=[out_spec, out_spec, pl.BlockSpec((1, tc), lambda o, c: (0, o * nct + c))],
        out_shape=[jax.ShapeDtypeStruct((L, HY_ORDER * W_HY), BF16), jax.ShapeDtypeStruct((L, HY_ORDER * W_HY), BF16),
                   jax.ShapeDtypeStruct((1, HY_ORDER * W_HY), F32)],
        scratch_shapes=[pltpu.VMEM((L, HY_FH), BF16), pltpu.VMEM((L, HY_FH), BF16)],
        compiler_params=_cparams(("arbitrary", "arbitrary")),
        name="hy_filt",
    )(emb_f, emb_b, t_f, t_b, w1, b1, w2, b2, w3, w3, decay, decay)


def _hy_spec_body(fa_ref, fb_ref, cz_ref, sz_ref, fre_ref, fim_ref, *, kb):
    cz, sz = cz_ref[...], sz_ref[...]
    a, b = fa_ref[...], fb_ref[...]
    kk = pl.program_id(1) * kb + lax.broadcasted_iota(jnp.int32, (kb, 1), 0)
    sgn = (1 - 2 * (kk & 1)).astype(F32)
    fre_ref[...] = (_dot(cz, a) + sgn * _dot(cz, b)).astype(BF16)
    fim_ref[...] = (-(_dot(sz, a) + sgn * _dot(sz, b))).astype(BF16)


def _hy_spec_call(L, kb, fa, fb, cz, sz):
    tc = 256
    n = fa.shape[1]
    return pl.pallas_call(
        functools.partial(_hy_spec_body, kb=kb),
        grid=(n // tc, L // kb),
        in_specs=[pl.BlockSpec((L, tc), lambda c, k: (0, c)), pl.BlockSpec((L, tc), lambda c, k: (0, c)),
                  pl.BlockSpec((kb, L), lambda c, k: (k, 0)), pl.BlockSpec((kb, L), lambda c, k: (k, 0))],
        out_specs=[pl.BlockSpec((kb, tc), lambda c, k: (k, c)), pl.BlockSpec((kb, tc), lambda c, k: (k, c))],
        out_shape=[jax.ShapeDtypeStruct((L, n), BF16), jax.ShapeDtypeStruct((L, n), BF16)],
        compiler_params=_cparams(("parallel", "parallel")),
        name="hy_spec",
    )(fa, fb, cz, sz)


def _hy_conv_body(v_ref, x1_ref, x2_ref, fre_ref, fim_ref, nyq_ref, bias_ref, cz_ref, sz_ref, czt_ref, szt_ref,
                  o_ref, z_ref, zb_ref, acc_ref, *, L, kb, nkb):
    o, k = pl.program_id(2), pl.program_id(3)
    inv_n = 1.0 / (2 * L)

    @pl.when((o == 0) & (k == 0))
    def _():
        z_ref[...] = v_ref[...]

    @pl.when(k == 0)
    def _():
        zb_ref[...] = z_ref[...].astype(BF16)
        acc_ref[...] = jnp.zeros_like(acc_ref)

    zb = zb_ref[...]
    zc = _dot(cz_ref[...], zb)
    zs = _dot(sz_ref[...], zb)
    fre, fim = fre_ref[...].astype(F32), fim_ref[...].astype(F32)
    kk = k * kb + lax.broadcasted_iota(jnp.int32, (kb, 1), 0)
    wk = jnp.where(kk == 0, inv_n, 2.0 * inv_n)
    yre = ((zc * fre + zs * fim) * wk).astype(BF16)
    yim = ((zc * fim - zs * fre) * wk).astype(BF16)
    acc_ref[...] += _dot(czt_ref[...], yre) - _dot(szt_ref[...], yim)

    @pl.when(k == nkb - 1)
    def _():
        z = z_ref[...]
        row = lax.broadcasted_iota(jnp.int32, (L, 1), 0)
        sgn = (1 - 2 * (row & 1)).astype(F32)
        z_nyq = jnp.sum(z * sgn, axis=0, keepdims=True)
        y = acc_ref[...] + sgn * (z_nyq * nyq_ref[...] * inv_n) + bias_ref[...] * z

        @pl.when(o == 0)
        def _():
            z_ref[...] = x1_ref[...] * y

        @pl.when(o == 1)
        def _():
            o_ref[...] = (x2_ref[...] * y).astype(BF16)


def _hy_conv_call(hu, fre, fim, nyq, bias, cz, sz, L, kb, tc, base, n_seq):
    nct = W_HY // tc
    nkb = L // kb

    def zin(part):
        return pl.BlockSpec((L, tc), lambda b, c, o, k: (base + b, part * nct + c))

    fspec = pl.BlockSpec((kb, tc), lambda b, c, o, k: (k, o * nct + c))
    vspec = pl.BlockSpec((1, tc), lambda b, c, o, k: (0, o * nct + c))
    return pl.pallas_call(
        functools.partial(_hy_conv_body, L=L, kb=kb, nkb=nkb),
        grid=(n_seq, nct, HY_ORDER, nkb),
        in_specs=[zin(0), zin(1), zin(2), fspec, fspec, vspec, vspec,
                  pl.BlockSpec((kb, L), lambda b, c, o, k: (k, 0)), pl.BlockSpec((kb, L), lambda b, c, o, k: (k, 0)),
                  pl.BlockSpec((L, kb), lambda b, c, o, k: (0, k)), pl.BlockSpec((L, kb), lambda b, c, o, k: (0, k))],
        out_specs=pl.BlockSpec((L, tc), lambda b, c, o, k: (b, c)),
        out_shape=jax.ShapeDtypeStruct((n_seq * L, W_HY), BF16),
        scratch_shapes=[pltpu.VMEM((L, tc), F32), pltpu.VMEM((L, tc), BF16), pltpu.VMEM((L, tc), F32)],
        compiler_params=_cparams(("parallel", "parallel", "arbitrary", "arbitrary")),
        name="hy_conv",
    )(hu, hu, hu, fre, fim, nyq, bias, cz, sz, cz, sz)


def _hyena_tables(L):
    t = np.arange(L, dtype=np.float32) / np.float32(L)

    def emb(tt):
        ang = (np.float32(2.0 * math.pi) * tt[:, None] * np.arange(1, HY_BANDS + 1, dtype=np.float32)[None, :]).astype(np.float32)
        e = np.concatenate([tt[:, None], np.cos(ang), np.sin(ang)], axis=-1).astype(np.float32)
        return np.pad(e, ((0, 0), (0, 128 - HY_EMB)))

    tb = np.concatenate([t[:1], t[:0:-1]])
    kk = np.arange(L, dtype=np.int64)
    ang = (2.0 * math.pi / (2 * L)) * ((kk[:, None] * kk[None, :]) % (2 * L)).astype(np.float64)
    return dict(emb_f=jnp.asarray(emb(t), BF16), emb_b=jnp.asarray(emb(tb), BF16),
                t_f=jnp.asarray(t[:, None]), t_b=jnp.asarray(tb[:, None]),
                cz=jnp.asarray(np.cos(ang), BF16), sz=jnp.asarray(np.sin(ang), BF16))


def _rope_tables(L):
    nf = ML_DH // 4
    freqs = (np.float32(ROPE_BASE) ** (-np.arange(nf, dtype=np.float32) / np.float32(nf))).astype(np.float32)
    pos = np.arange(L)
    out_c, out_s = [], []
    for p in (pos // GRID_W, pos % GRID_W):
        ang = p.astype(np.float32)[:, None] * freqs[None, :]
        c, s = np.cos(ang).astype(np.float32), np.sin(ang).astype(np.float32)
        out_c += [c, c]
        out_s += [-s, s]
    return jnp.asarray(np.concatenate(out_c, axis=-1)), jnp.asarray(np.concatenate(out_s, axis=-1))


def kernel(x_prompt, x_sample, state_rglru, state_ssd, state_mlstm_C, state_mlstm_n, state_mlstm_m, c, c_ctx,
           ada_w, ada_b, ln_g, ln_b, ffn_wg, ffn_wu, ffn_wd, w_in, rg_conv_w, rg_conv_b, rg_gate_w, rg_gate_b,
           rg_lambda, hy_conv_w, hy_conv_b, hy_w1, hy_b1, hy_w2, hy_b2, hy_w3, hy_decay, hy_bias,
           ssd_conv_w, ssd_conv_b, ssd_dt_bias, ssd_A_log, ssd_D, ssd_norm_w, ml_gate_b, ml_norm_w,
           branch_w, mix_out):
    bp, lp_len = x_prompt.shape[:2]
    bs, ls = x_sample.shape[:2]
    mp, ms = bp * lp_len, bs * ls
    ncp, ncs = lp_len // CHUNK, ls // CHUNK

    def midx(i, bm):
        return jnp.where(i < mp // bm, 0, 1 + (i - mp // bm) // (ls // bm))

    cond = jnp.concatenate([c_ctx[None], c, jnp.zeros((8 - 1 - bs, D_MODEL), F32)], axis=0)
    s = jax.nn.silu(cond).astype(BF16)
    mod = _ada_call(s, ada_w, ada_b.reshape(DEPTH, 1, N_MOD * D_MODEL))
    mod = mod[:, :1 + bs].reshape(DEPTH, 1 + bs, N_MOD, 1, D_MODEL)

    def mod_row(l, j):
        return mod[l, :, j]

    o = IN_OFFSETS

    w_t = jnp.swapaxes(w_in, 1, 2)

    def wseg(a, b):
        return w_t[:, a:b]

    w_small = jnp.concatenate(
        [wseg(o[1], o[2]), wseg(0, o[1]), wseg(o[2], o[4]),
         wseg(o[4], o[5]), wseg(o[6], o[7]),
         jnp.zeros((DEPTH, N_PROJ - O_SMALL - 2 * SSD_H - 4 * ML_NH, D_MODEL), F32)], axis=1).astype(BF16)
    w_qkvo = wseg(o[5], o[6]).astype(BF16)
    w_merge = wseg(o[7], w_in.shape[-1]).astype(BF16)
    wg00, wu00 = ffn_wg[0, 0].astype(BF16), ffn_wu[0, 0].astype(BF16)
    bw2d = branch_w.reshape(DEPTH, N_BRANCH * W_BRANCH, D_MODEL)
    gw_b = rg_gate_w.astype(BF16)
    hw1 = jnp.pad(hy_w1, ((0, 0), (0, 128 - HY_EMB), (0, 0))).astype(BF16)
    hw2, hw3 = hy_w2.astype(BF16), hy_w3.astype(BF16)

    tabs_p, tabs_s = _hyena_tables(lp_len), _hyena_tables(ls)
    rope_s = _rope_tables(ls)

    x = jnp.concatenate([x_prompt.reshape(mp, D_MODEL), x_sample.reshape(ms, D_MODEL)], axis=0)
    h = _modulate_call(x, mod_row(0, 0), mod_row(0, 1), midx)

    new_states = []
    wg1, wu1 = wg00, wu00
    for l in range(DEPTH):
        lng = ln_g[l].reshape(3, 1, D_MODEL)
        lnb = ln_b[l].reshape(3, 1, D_MODEL)

        a, (wd1,) = _ffn_up_call(h, wg1, wu1, (), sides=[(ffn_wd, (l, 0))])
        x, h = _res_ln_call(a, wd1, (), x, mod_row(l, 2), lng[0], lnb[0], mod_row(l, 3), mod_row(l, 4),
                            midx, 0.5, True)

        proj, (bw_l, mo_l) = _in_proj_call(h, w_small, (l,), sides=[(bw2d, (l,)), (mix_out, (l,))])
        proj_q, _ = _in_proj_call(h, w_qkvo, (l,))
        hu, xa, xbc = _conv_call(proj, hy_conv_w[l], hy_conv_b[l][None], rg_conv_w[l], rg_conv_b[l][None],
                                 ssd_conv_w[l], ssd_conv_b[l][None], mp // CHUNK, ncp, ncs)

        gb = rg_gate_b[l].reshape(4, W_RG)
        a_p, rg_last = _rglru_call(xa, proj, gw_b[l], gb, rg_lambda[l], None, 0, bp, ncp)
        a_s, _ = _rglru_call(xa, proj, gw_b[l], gb, rg_lambda[l], state_rglru[:, l], mp // CHUNK, bs, ncs)

        outs_b = []
        for tabs, L, kb, tc, base, n_seq in ((tabs_p, lp_len, lp_len, W_HY, 0, bp), (tabs_s, ls, 256, 256, mp // ls, bs)):
            fa, fb, nyq = _hy_filt_call(L, tabs['emb_f'], tabs['emb_b'], tabs['t_f'], tabs['t_b'], hw1[l], hy_b1[l][None],
                                        hw2[l], hy_b2[l][None], hw3[l], hy_decay[l][None])
            fre, fim = _hy_spec_call(L, kb, fa, fb, tabs['cz'], tabs['sz'])
            outs_b.append(_hy_conv_call(hu, fre, fim, nyq, hy_bias[l].reshape(1, HY_ORDER * W_HY), tabs['cz'], tabs['sz'],
                                        L, kb, tc, base, n_seq))

        dvec = jnp.repeat(ssd_D[l], SSD_P)[None]
        nw = ssd_norm_w[l][None]
        c_p, ssd_last = _ssd_call(xbc, proj, ssd_dt_bias[l], ssd_A_log[l], dvec, nw, None, 0, bp, ncp)
        c_s, _ = _ssd_call(xbc, proj, ssd_dt_bias[l], ssd_A_log[l], dvec, nw, state_ssd[:, l], mp // CHUNK, bs, ncs)

        mnw = ml_norm_w[l][None]
        d_p, cl, nl, ml = _mlstm_call(proj, proj_q, ml_gate_b[l], mnw, None, None, 0, bp, ncp)
        d_s, _, _, _ = _mlstm_call(proj, proj_q, ml_gate_b[l], mnw, rope_s,
                                   (state_mlstm_C[:, l], state_mlstm_n[:, l], state_mlstm_m[:, l]), mp // CHUNK, bs, ncs)
        new_states.append((rg_last, ssd_last, cl, nl, ml))

        merged, (wg2, wu2, wd2) = _merge_call(h, (a_p, outs_b[0], c_p, d_p), (a_s, outs_b[1], c_s, d_s), w_merge, (l,), bw_l,
                                              sides=[(ffn_wg, (l, 1)), (ffn_wu, (l, 1)), (ffn_wd, (l, 1))])
        x, h = _res_ln_call(merged, mo_l, (), x, mod_row(l, 5), lng[1], lnb[1], mod_row(l, 6), mod_row(l, 7),
                            midx, 1.0, True)

        last = l == DEPTH - 1
        a, cast = _ffn_up_call(h, wg2, wu2, (), sides=[] if last else [(ffn_wg, (l + 1, 0)), (ffn_wu, (l + 1, 0))])
        if not last:
            wg1, wu1 = cast
        nl_ = l if last else l + 1
        x, h = _res_ln_call(a, wd2, (), x, mod_row(l, 8), lng[2], lnb[2], mod_row(nl_, 0), mod_row(nl_, 1),
                            midx, 0.5, not last)

    yp = x[:mp].reshape(bp, lp_len, D_MODEL)
    ys = x[mp:].reshape(bs, ls, D_MODEL)
    outs = tuple(jnp.stack([new_states[l][j] for l in range(DEPTH)], axis=1) for j in range(5))
    return (yp, ys) + outs
```

```python
import functools
import math

import jax
import jax.numpy as jnp
import numpy as np
from jax import lax
from jax.experimental import pallas as pl
from jax.experimental.pallas import tpu as pltpu

D_MODEL = 4096
DEPTH = 2
GRID_W = 64
N_BRANCH = 4
W_BRANCH = D_MODEL // N_BRANCH
D_FF = 2 * D_MODEL
N_MOD = 9
CHUNK = 128
W_RG = W_BRANCH
RG_BLOCKS = 8
RG_BW = W_RG // RG_BLOCKS
RG_C = 8.0
W_HY = W_BRANCH
HY_ORDER = 2
HY_BANDS = 16
HY_EMB = 2 * HY_BANDS + 1
HY_FH = 64
SSD_DI = W_BRANCH
SSD_P = 64
SSD_H = SSD_DI // SSD_P
SSD_N = 128
SSD_G = 2
SSD_CONV_CH = SSD_DI + 2 * SSD_G * SSD_N
W_ML = W_BRANCH
ML_NH = 4
ML_DH = W_ML // ML_NH
ROPE_BASE = 10000.0
ALPHA = (2 * DEPTH) ** 0.25

IN_SIZES = (W_RG, W_RG, 3 * W_HY, SSD_DI, SSD_CONV_CH, 2 * SSD_H, 4 * W_ML, 4 * ML_NH, N_BRANCH * D_MODEL)
IN_OFFSETS = tuple(int(s) for s in np.cumsum(IN_SIZES)[:-1])

F32 = jnp.float32
BF16 = jnp.bfloat16
HI = lax.Precision.HIGHEST
NT = (((1,), (1,)), ((), ()))
TN = (((0,), (0,)), ((), ()))

O_HY = 0
O_RGX = O_HY + 3 * W_HY
O_RGY = O_RGX + W_RG
O_SSDZ = O_RGY + W_RG
O_XBC = O_SSDZ + SSD_DI
O_SMALL = O_XBC + SSD_CONV_CH
W_SMALL = 128
N_PROJ = 8192

VMEM_LIMIT = 60 * 1024 * 1024


def _cparams(sem):
    return pltpu.CompilerParams(dimension_semantics=sem, vmem_limit_bytes=VMEM_LIMIT)


def _dot(a, b):
    return jnp.dot(a, b, preferred_element_type=F32)


def _dg(a, b, dims, precision=None):
    return lax.dot_general(a, b, dims, preferred_element_type=F32, precision=precision)


def _ada_body(s_ref, w_ref, b_ref, o_ref):
    o_ref[...] = _dot(s_ref[...], w_ref[...].astype(BF16)) + b_ref[...]


def _ada_call(s, ada_w, ada_b):
    bn = 512
    n = ada_w.shape[-1]
    return pl.pallas_call(
        _ada_body,
        grid=(DEPTH, n // bn),
        in_specs=[pl.BlockSpec((8, D_MODEL), lambda l, j: (0, 0)),
                  pl.BlockSpec((None, D_MODEL, bn), lambda l, j: (l, 0, j)),
                  pl.BlockSpec((None, 1, bn), lambda l, j: (l, 0, j))],
        out_specs=pl.BlockSpec((None, 8, bn), lambda l, j: (l, 0, j)),
        out_shape=jax.ShapeDtypeStruct((DEPTH, 8, n), F32),
        compiler_params=_cparams(("parallel", "parallel")),
        name="ada_mod",
    )(s, ada_w, ada_b)


def _modulate_body(x_ref, sh_ref, sc_ref, o_ref):
    o_ref[...] = (x_ref[...] * (1.0 + sc_ref[...]) + sh_ref[...]).astype(BF16)


def _modulate_call(x, shift, scale, midx):
    m = x.shape[0]
    bm = 256
    mspec = pl.BlockSpec((None, 1, D_MODEL), lambda i: (midx(i, bm), 0, 0))
    return pl.pallas_call(
        _modulate_body,
        grid=(m // bm,),
        in_specs=[pl.BlockSpec((bm, D_MODEL), lambda i: (i, 0)), mspec, mspec],
        out_specs=pl.BlockSpec((bm, D_MODEL), lambda i: (i, 0)),
        out_shape=jax.ShapeDtypeStruct((m, D_MODEL), BF16),
        compiler_params=_cparams(("parallel",)),
        name="modulate",
    )(x, shift, scale)


def _ffn_up_body(h_ref, wg_ref, wu_ref, o_ref):
    h = h_ref[...]
    g = _dot(h, wg_ref[...])
    u = _dot(h, wu_ref[...])
    o_ref[...] = (g * jax.nn.sigmoid(g) * u).astype(BF16)


def _wspec(lead, block, index_fn):
    lead = tuple(lead)
    return pl.BlockSpec((None,) * len(lead) + tuple(block), lambda *g: lead + tuple(index_fn(*g)))


def _side_plan(sides, grid):
    n_steps = grid[0] * grid[1]
    ins, outs, shapes = [], [], []
    for src, lead in sides:
        r, c = src.shape[-2:]
        rb = next(b for b in (16, 32, 64, 128, 256, 512) if r % b == 0 and r // b <= n_steps)
        nblk = r // rb

        def blk(i, j, nblk=nblk):
            return jnp.minimum(i * grid[1] + j, nblk - 1)

        ins.append(_wspec(lead, (rb, c), lambda i, j, blk=blk: (blk(i, j), 0)))
        outs.append(pl.BlockSpec((rb, c), lambda i, j, blk=blk: (blk(i, j), 0)))
        shapes.append(jax.ShapeDtypeStruct((r, c), BF16))
    return ins, outs, shapes


def _with_sides(body, n_in, n_out, n_side):
    def wrapped(*refs):
        ins, rest = refs[:n_in], refs[n_in:]
        side_src, rest = rest[:n_side], rest[n_side:]
        outs, rest = rest[:n_out], rest[n_out:]
        side_dst, scratch = rest[:n_side], rest[n_side:]
        body(*ins, *outs, *scratch)
        for s, d in zip(side_src, side_dst):
            d[...] = s[...].astype(BF16)
    return wrapped


def _ffn_up_call(h, wg, wu, lead, sides=()):
    m = h.shape[0]
    bm, bn = 1024, 512
    grid = (m // bm, D_FF // bn)
    s_in, s_out, s_shape = _side_plan(sides, grid)
    res = pl.pallas_call(
        _with_sides(_ffn_up_body, 3, 1, len(sides)),
        grid=grid,
        in_specs=[pl.BlockSpec((bm, D_MODEL), lambda i, j: (i, 0)),
                  _wspec(lead, (D_MODEL, bn), lambda i, j: (0, j)),
                  _wspec(lead, (D_MODEL, bn), lambda i, j: (0, j))] + s_in,
        out_specs=[pl.BlockSpec((bm, bn), lambda i, j: (i, j))] + s_out,
        out_shape=[jax.ShapeDtypeStruct((m, D_FF), BF16)] + s_shape,
        compiler_params=_cparams(("arbitrary", "arbitrary")),
        name="ffn_up",
    )(h, wg, wu, *[s for s, _ in sides])
    return res[0], res[1:]


def _in_proj_body(h_ref, wt_ref, o_ref):
    o_ref[...] = _dg(h_ref[...], wt_ref[...], NT)


def _in_proj_call(h, wt, lead, sides=()):
    m = h.shape[0]
    n = wt.shape[-2]
    bm, bn = 1024, 512
    grid = (m // bm, n // bn)
    s_in, s_out, s_shape = _side_plan(sides, grid)
    res = pl.pallas_call(
        _with_sides(_in_proj_body, 2, 1, len(sides)),
        grid=grid,
        in_specs=[pl.BlockSpec((bm, D_MODEL), lambda i, j: (i, 0)),
                  _wspec(lead, (bn, D_MODEL), lambda i, j: (j, 0))] + s_in,
        out_specs=[pl.BlockSpec((bm, bn), lambda i, j: (i, j))] + s_out,
        out_shape=[jax.ShapeDtypeStruct((m, n), F32)] + s_shape,
        compiler_params=_cparams(("arbitrary", "arbitrary")),
        name="in_proj",
    )(h, wt, *[s for s, _ in sides])
    return res[0], res[1:]


def _merge_body(*refs, p_blocks):
    h_ref, bp, bs, wg, wb, o_ref = refs[0], refs[1:5], refs[5:9], refs[9:13], refs[13:17], refs[17]
    h = h_ref[...]
    is_prompt = pl.program_id(0) < p_blocks
    acc = None
    for j in range(N_BRANCH):
        br = jnp.where(is_prompt, bp[j][...], bs[j][...])
        t = jax.nn.sigmoid(_dg(h, wg[j][...], NT)) * _dot(br, wb[j][...])
        acc = t if acc is None else acc + t
    o_ref[...] = acc.astype(BF16)


def _merge_call(h, br_prompt, br_sample, w_merge, lead, branch_w, sides=()):
    m = h.shape[0]
    bm, bn = 512, 256
    nb = D_MODEL // bn
    grid = (m // bm, nb)
    p_blocks = br_prompt[0].shape[0] // bm
    bp_specs = [pl.BlockSpec((bm, W_BRANCH), lambda i, n: (jnp.minimum(i, p_blocks - 1), 0)) for _ in range(N_BRANCH)]
    bs_specs = [pl.BlockSpec((bm, W_BRANCH), lambda i, n: (jnp.maximum(i - p_blocks, 0), 0)) for _ in range(N_BRANCH)]
    wg_specs = [_wspec(lead, (bn, D_MODEL), functools.partial(lambda i, n, j: (j * nb + n, 0), j=j))
                for j in range(N_BRANCH)]
    wb_specs = [pl.BlockSpec((W_BRANCH, bn), functools.partial(lambda i, n, j: (j, n), j=j)) for j in range(N_BRANCH)]
    s_in, s_out, s_shape = _side_plan(sides, grid)
    res = pl.pallas_call(
        _with_sides(functools.partial(_merge_body, p_blocks=p_blocks), 1 + 4 * N_BRANCH, 1, len(sides)),
        grid=grid,
        in_specs=[pl.BlockSpec((bm, D_MODEL), lambda i, n: (i, 0))] + bp_specs + bs_specs + wg_specs + wb_specs + s_in,
        out_specs=[pl.BlockSpec((bm, bn), lambda i, n: (i, n))] + s_out,
        out_shape=[jax.ShapeDtypeStruct((m, D_MODEL), BF16)] + s_shape,
        compiler_params=_cparams(("arbitrary", "arbitrary")),
        name="merge",
    )(h, *br_prompt, *br_sample, *([w_merge] * N_BRANCH), *([branch_w] * N_BRANCH), *[s for s, _ in sides])
    return res[0], res[1:]


def _res_ln_body(a_ref, w_ref, x_ref, gate_ref, lng_ref, lnb_ref, nsh_ref, nsc_ref, y_ref, hn_ref, st_ref, *,
                 coef, nk, bm, emit_next):
    k = pl.program_id(1)

    @pl.when(k == 0)
    def _():
        y_ref[...] = _dot(a_ref[...], w_ref[...])

    @pl.when(k > 0)
    def _():
        y_ref[...] += _dot(a_ref[...], w_ref[...])

    @pl.when(k == nk - 1)
    def _():
        rows = 8
        gate = coef * gate_ref[...]
        lng, lnb = lng_ref[...], lnb_ref[...]
        nsc, nsh = 1.0 + nsc_ref[...], nsh_ref[...]

        def rows_of(c):
            return pl.ds(pl.multiple_of(c * rows, rows), rows)

        def sweep_mean(c, carry):
            rs = rows_of(c)
            r = ALPHA * x_ref[rs, :] + gate * y_ref[rs, :]
            y_ref[rs, :] = r
            st_ref[rs, 0:1] = jnp.mean(r, axis=-1, keepdims=True)
            return carry

        def sweep_var(c, carry):
            rs = rows_of(c)
            d = y_ref[rs, :] - st_ref[rs, 0:1]
            st_ref[rs, 1:2] = lax.rsqrt(jnp.mean(d * d, axis=-1, keepdims=True) + 1e-5)
            return carry

        def sweep_out(c, carry):
            rs = rows_of(c)
            o = (y_ref[rs, :] - st_ref[rs, 0:1]) * st_ref[rs, 1:2] * lng + lnb
            y_ref[rs, :] = o
            if emit_next:
                hn_ref[rs, :] = (o * nsc + nsh).astype(BF16)
            return carry

        for sweep, unroll in ((sweep_mean, 8), (sweep_var, 8), (sweep_out, 4)):
            lax.fori_loop(0, bm // rows, sweep, 0, unroll=unroll)
        if not emit_next:
            hn_ref[...] = jnp.zeros_like(hn_ref)


def _res_ln_call(a, w, lead, x, gate, lng, lnb, nshift, nscale, midx, coef, emit_next):
    m, kdim = a.shape
    bm, bk = 512, 1024
    nk = kdim // bk
    hn_rows = bm if emit_next else 8
    mspec = pl.BlockSpec((None, 1, D_MODEL), lambda i, k: (midx(i, bm), 0, 0))
    vspec = pl.BlockSpec((1, D_MODEL), lambda i, k: (0, 0))
    y, hn = pl.pallas_call(
        functools.partial(_res_ln_body, coef=coef, nk=nk, bm=bm, emit_next=emit_next),
        grid=(m // bm, nk),
        in_specs=[pl.BlockSpec((bm, bk), lambda i, k: (i, k)),
                  _wspec(lead, (bk, D_MODEL), lambda i, k: (k, 0)),
                  pl.BlockSpec((bm, D_MODEL), lambda i, k: (i, 0)),
                  mspec, vspec, vspec, mspec, mspec],
        out_specs=[pl.BlockSpec((bm, D_MODEL), lambda i, k: (i, 0)),
                   pl.BlockSpec((hn_rows, D_MODEL), lambda i, k: (i, 0))],
        out_shape=[jax.ShapeDtypeStruct((m, D_MODEL), F32),
                   jax.ShapeDtypeStruct((m if emit_next else 8 * (m // bm), D_MODEL), BF16)],
        scratch_shapes=[pltpu.VMEM((bm, 128), F32)],
        compiler_params=_cparams(("parallel", "arbitrary")),
        name="res_ln",
    )(a, w, x, gate, lng, lnb, nshift, nscale)
    return y, hn


def _conv_piece(pad_ref, w_ref, b_ref, o_ref, c0, width, taps, pad_l, act):
    cs = slice(c0, c0 + width)
    acc = b_ref[:, cs]
    for j in range(taps):
        r0 = 8 - pad_l + j
        acc = acc + w_ref[j:j + 1, cs] * pad_ref[r0:r0 + CHUNK, cs]
    o_ref[:, cs] = act(acc)


def _conv_body(hy_p, hy_c, hy_n, rg_p, rg_c, rg_n, xb_p, xb_c, xb_n,
               hw_ref, hb_ref, rw_ref, rb_ref, xw_ref, xb_ref,
               hu_ref, xa_ref, xbc_ref, pad_ref, *, chunk_pos):
    first, last = chunk_pos(pl.program_id(0))
    for prev, cur, nxt, w_ref, b_ref, o_ref, width, taps, pad_l, act in (
            (hy_p, hy_c, hy_n, hw_ref, hb_ref, hu_ref, 3 * W_HY, 3, 1, lambda v: v),
            (rg_p, rg_c, rg_n, rw_ref, rb_ref, xa_ref, W_RG, 4, 2, lambda v: v),
            (xb_p, xb_c, xb_n, xw_ref, xb_ref, xbc_ref, SSD_CONV_CH, 4, 2, jax.nn.silu)):
        pad_ref[0:8, 0:width] = jnp.where(first, 0.0, prev[...])
        pad_ref[8:8 + CHUNK, 0:width] = cur[...]
        pad_ref[8 + CHUNK:16 + CHUNK, 0:width] = jnp.where(last, 0.0, nxt[...])
        for c0 in range(0, width, 512):
            _conv_piece(pad_ref, w_ref, b_ref, o_ref, c0, min(512, width - c0), taps, pad_l, act)


def _conv_call(proj, hy_w, hy_b, rg_w, rg_b, xb_w, xb_b, mp_chunks, ncp, ncs):
    m = proj.shape[0]
    n8 = m // 8

    def chunk_pos(i):
        pos = jnp.where(i < mp_chunks, i % ncp, (i - mp_chunks) % ncs)
        n = jnp.where(i < mp_chunks, ncp, ncs)
        return pos == 0, pos == n - 1

    def trio(width, col):
        return [pl.BlockSpec((8, width), lambda i: (jnp.maximum(i * (CHUNK // 8) - 1, 0), col)),
                pl.BlockSpec((CHUNK, width), lambda i: (i, col)),
                pl.BlockSpec((8, width), lambda i: (jnp.minimum((i + 1) * (CHUNK // 8), n8 - 1), col))]

    def full(a):
        return pl.BlockSpec(a.shape, lambda i: (0, 0))

    params = (hy_w, hy_b, rg_w, rg_b, xb_w, xb_b)
    return pl.pallas_call(
        functools.partial(_conv_body, chunk_pos=chunk_pos),
        grid=(m // CHUNK,),
        in_specs=trio(3 * W_HY, O_HY // (3 * W_HY)) + trio(W_RG, O_RGX // W_RG)
        + trio(SSD_CONV_CH, O_XBC // SSD_CONV_CH) + [full(a) for a in params],
        out_specs=[pl.BlockSpec((CHUNK, 3 * W_HY), lambda i: (i, 0)),
                   pl.BlockSpec((CHUNK, W_RG), lambda i: (i, 0)),
                   pl.BlockSpec((CHUNK, SSD_CONV_CH), lambda i: (i, 0))],
        out_shape=[jax.ShapeDtypeStruct((m, 3 * W_HY), F32),
                   jax.ShapeDtypeStruct((m, W_RG), F32),
                   jax.ShapeDtypeStruct((m, SSD_CONV_CH), F32)],
        scratch_shapes=[pltpu.VMEM((CHUNK + 16, 3 * W_HY), F32)],
        compiler_params=_cparams(("parallel",)),
        name="dw_conv",
    )(*([proj] * 9), *params)


def _chunk_of(j, nc):
    return jnp.where(j < nc, j, 2 * nc - 1 - j)


def _row_map(base, nc, col):
    return lambda b, j: (base + b * nc + _chunk_of(j, nc), col)


def _out_map(nc):
    return lambda b, j: (b * nc + jnp.where(j < nc, nc - 1, 2 * nc - 1 - j), 0)


def _tri(rev):
    r = lax.broadcasted_iota(jnp.int32, (CHUNK, CHUNK), 0)
    c = lax.broadcasted_iota(jnp.int32, (CHUNK, CHUNK), 1)
    return (c >= r) if rev else (c <= r)


def _two_phase(nc, run):
    j = pl.program_id(1)

    @pl.when(j < nc)
    def _():
        run(0, j == 0, j == nc - 1, j)

    @pl.when(j >= nc)
    def _():
        run(1, j == nc, j == 2 * nc - 1, 2 * nc - 1 - j)


def _rglru_body(xa_ref, y_ref, gw_ref, gb_ref, lam_ref, h0_ref, o_ref, hl_ref,
                yf_ref, a_ref, u_ref, hb_ref, hc_ref, *, nc, init):
    def run(d, is_first, is_last, ci):
        x = xa_ref[...]
        xb = x.astype(BF16)
        sp = jax.nn.softplus(-lam_ref[d:d + 1, :])
        for n in range(RG_BLOCKS):
            cs = slice(n * RG_BW, (n + 1) * RG_BW)
            r = jax.nn.sigmoid(_dot(xb[:, cs], gw_ref[d, 0, n]) + gb_ref[2 * d:2 * d + 1, cs])
            i = jax.nn.sigmoid(_dot(xb[:, cs], gw_ref[d, 1, n]) + gb_ref[2 * d + 1:2 * d + 2, cs])
            log_a = -RG_C * r * sp[:, cs]
            th = jnp.tanh(log_a)
            a_ref[:, cs] = jnp.exp(log_a)
            u_ref[:, cs] = jnp.sqrt(-2.0 * th / (1.0 - th)) * (i * x[:, cs])

        @pl.when(is_first)
        def _():
            hc_ref[0:1, :] = h0_ref[d:d + 1, :] if init else jnp.zeros((1, W_RG), F32)

        def step(s, h):
            t = (CHUNK - 1 - s) if d else s
            h = a_ref[pl.ds(t, 1), :] * h + u_ref[pl.ds(t, 1), :]
            hb_ref[pl.ds(t, 1), :] = h
            return h

        h = lax.fori_loop(0, CHUNK, step, hc_ref[0:1, :], unroll=8)
        hc_ref[0:1, :] = h

        @pl.when(is_last)
        def _():
            hl_ref[d:d + 1, :] = h

        r0 = pl.multiple_of(ci * CHUNK, CHUNK)
        if d == 0:
            yf_ref[pl.ds(r0, CHUNK), :] = hb_ref[...]
        else:
            o_ref[...] = ((yf_ref[pl.ds(r0, CHUNK), :] + hb_ref[...]) * jax.nn.gelu(y_ref[...])).astype(BF16)

    _two_phase(nc, run)


def _rglru_call(xa, proj, gw, gb, lam, h0, base, n_seq, nc):
    init = h0 is not None
    if not init:
        h0 = jnp.zeros((n_seq, 2, W_RG), F32)
    L = nc * CHUNK
    st_spec = pl.BlockSpec((None, 2, W_RG), lambda b, j: (b, 0, 0))
    return pl.pallas_call(
        functools.partial(_rglru_body, nc=nc, init=init),
        grid=(n_seq, 2 * nc),
        in_specs=[pl.BlockSpec((CHUNK, W_RG), _row_map(base, nc, 0)),
                  pl.BlockSpec((CHUNK, W_RG), _row_map(base, nc, O_RGY // W_RG)),
                  pl.BlockSpec(gw.shape, lambda b, j: (0, 0, 0, 0, 0)),
                  pl.BlockSpec(gb.shape, lambda b, j: (0, 0)),
                  pl.BlockSpec(lam.shape, lambda b, j: (0, 0)),
                  st_spec],
        out_specs=[pl.BlockSpec((CHUNK, W_RG), _out_map(nc)), st_spec],
        out_shape=[jax.ShapeDtypeStruct((n_seq * L, W_RG), BF16),
                   jax.ShapeDtypeStruct((n_seq, 2, W_RG), F32)],
        scratch_shapes=[pltpu.VMEM((L, W_RG), F32), pltpu.VMEM((CHUNK, W_RG), F32), pltpu.VMEM((CHUNK, W_RG), F32),
                        pltpu.VMEM((CHUNK, W_RG), F32), pltpu.VMEM((8, W_RG), F32)],
        compiler_params=_cparams(("parallel", "arbitrary")),
        name="rglru",
    )(xa, proj, gw, gb, lam, h0)


def _ssd_body(xbc_ref, z_ref, sm_ref, dtb_r, dtb_c, al_r, al_c, dvec_ref, nw_ref, s0_ref, o_ref, sl_ref,
              yf_ref, yb_ref, s_ref, *, nc, init):
    def run(d, is_first, is_last, ci):
        hs = slice(SSD_H * d, SSD_H * (d + 1))
        sm = sm_ref[...]
        smt = sm.T
        dt_c = jax.nn.softplus(sm[:, hs] + dtb_r[:, hs])
        dt_r = jax.nn.softplus(smt[hs, :] + dtb_c[hs, :])
        mask = _tri(d)
        tri = jnp.where(mask, 1.0, 0.0).astype(F32)
        cs_c = jnp.dot(tri, dt_c * -jnp.exp(al_r[:, hs]), preferred_element_type=F32, precision=HI)
        cs_r = _dg(dt_r * -jnp.exp(al_c[hs, :]), tri, NT, HI)
        e = 0 if d else CHUNK - 1
        tot = cs_c[e:e + 1, :]
        w_end = jnp.exp(tot - cs_c) * dt_c
        ecs = jnp.exp(cs_c)
        etot = jnp.exp(tot)

        @pl.when(is_first)
        def _():
            s_ref[...] = s0_ref[d] if init else jnp.zeros(s_ref.shape, F32)

        xs = xbc_ref[:, 0:SSD_DI]
        xsb = xs.astype(BF16)
        for g in range(SSD_G):
            bb = xbc_ref[:, SSD_DI + g * SSD_N:SSD_DI + (g + 1) * SSD_N].astype(BF16)
            cb = xbc_ref[:, SSD_DI + (SSD_G + g) * SSD_N:SSD_DI + (SSD_G + g + 1) * SSD_N].astype(BF16)
            gmat = _dg(cb, bb, NT)
            for hh in range(SSD_H // SSD_G):
                h = g * (SSD_H // SSD_G) + hh
                ps = slice(h * SSD_P, (h + 1) * SSD_P)
                seg = cs_c[:, h:h + 1] - cs_r[h:h + 1, :]
                dec = jnp.exp(jnp.where(mask, seg, -jnp.inf))
                mh = (gmat * dec * dt_r[h:h + 1, :]).astype(BF16)
                sh = s_ref[h]
                yb_ref[:, ps] = _dot(mh, xsb[:, ps]) + _dg(cb, sh.astype(BF16), NT) * ecs[:, h:h + 1]
                xw = (xs[:, ps] * w_end[:, h:h + 1]).astype(BF16)
                s_ref[h] = etot[:, h:h + 1] * sh + _dg(xw, bb, TN)

        @pl.when(is_last)
        def _():
            sl_ref[d] = s_ref[...]

        r0 = pl.multiple_of(ci * CHUNK, CHUNK)
        if d == 0:
            yf_ref[pl.ds(r0, CHUNK), :] = yb_ref[...]
        else:
            yc = yf_ref[pl.ds(r0, CHUNK), :] + yb_ref[...] + dvec_ref[...] * xs
            v = yc * jax.nn.silu(z_ref[...])
            o_ref[...] = (v * lax.rsqrt(jnp.mean(v * v, axis=-1, keepdims=True) + 1e-6) * nw_ref[...]).astype(BF16)

    _two_phase(nc, run)


def _ssd_call(xbc, proj, dt_bias, a_log, dvec, norm_w, s0, base, n_seq, nc):
    init = s0 is not None
    if not init:
        s0 = jnp.zeros((n_seq, 2, 8, 8, SSD_N), F32)
    L = nc * CHUNK
    st_shape = (SSD_H, SSD_P, SSD_N)
    dtb_r, al_r = dt_bias.reshape(1, 2 * SSD_H), a_log.reshape(1, 2 * SSD_H)
    dtb_c, al_c = dt_bias.reshape(2 * SSD_H, 1), a_log.reshape(2 * SSD_H, 1)

    def full(a):
        return pl.BlockSpec(a.shape, lambda b, j: (0,) * a.ndim)

    s0_spec = pl.BlockSpec((None,) + s0.shape[1:], lambda b, j: (b, 0, 0, 0, 0))
    return pl.pallas_call(
        functools.partial(_ssd_body, nc=nc, init=init),
        grid=(n_seq, 2 * nc),
        in_specs=[pl.BlockSpec((CHUNK, SSD_CONV_CH), _row_map(base, nc, 0)),
                  pl.BlockSpec((CHUNK, SSD_DI), _row_map(base, nc, O_SSDZ // SSD_DI)),
                  pl.BlockSpec((CHUNK, W_SMALL), _row_map(base, nc, O_SMALL // W_SMALL)),
                  full(dtb_r), full(dtb_c), full(al_r), full(al_c), full(dvec), full(norm_w), s0_spec],
        out_specs=[pl.BlockSpec((CHUNK, SSD_DI), _out_map(nc)),
                   pl.BlockSpec((None, 2) + st_shape, lambda b, j: (b, 0, 0, 0, 0))],
        out_shape=[jax.ShapeDtypeStruct((n_seq * L, SSD_DI), BF16),
                   jax.ShapeDtypeStruct((n_seq, 2) + st_shape, F32)],
        scratch_shapes=[pltpu.VMEM((L, SSD_DI), F32), pltpu.VMEM((CHUNK, SSD_DI), F32), pltpu.VMEM(st_shape, F32)],
        compiler_params=_cparams(("parallel", "arbitrary")),
        name="ssd",
    )(xbc, proj, proj, dtb_r, dtb_c, al_r, al_c, dvec, norm_w, s0)


def _mlstm_body(qkvo_ref, sm_ref, gb_r, gb_c, nw_ref, cos_ref, sin_ref, c0_ref, n0_ref, m0_ref,
                o_ref, cl_ref, nl_ref, ml_ref, hf_ref, hb_ref, c_ref, n_ref, m_ref, *, nc, init, rope):
    def run(d, is_first, is_last, ci):
        sm = sm_ref[...]
        smt = sm.T
        o_i, o_f = 2 * SSD_H + 8 * d, 2 * SSD_H + 8 * d + ML_NH
        i_c = sm[:, o_i:o_i + ML_NH] + gb_r[:, 8 * d:8 * d + ML_NH]
        i_r = smt[o_i:o_i + ML_NH, :] + gb_c[8 * d:8 * d + ML_NH, :]
        lf_c = jax.nn.log_sigmoid(sm[:, o_f:o_f + ML_NH] + gb_r[:, 8 * d + ML_NH:8 * d + 2 * ML_NH])
        lf_r = jax.nn.log_sigmoid(smt[o_f:o_f + ML_NH, :] + gb_c[8 * d + ML_NH:8 * d + 2 * ML_NH, :])
        mask = _tri(d)
        tri = jnp.where(mask, 1.0, 0.0).astype(F32)
        b_c = jnp.dot(tri, lf_c, preferred_element_type=F32, precision=HI)
        b_r = _dg(lf_r, tri, NT, HI)
        e = 0 if d else CHUNK - 1
        b_end = b_c[e:e + 1, :]

        @pl.when(is_first)
        def _():
            if init:
                c_ref[...] = c0_ref[d]
                n_ref[0:ML_NH, :] = n0_ref[d]
                m_ref[0:1, 0:ML_NH] = m0_ref[d:d + 1, :]
            else:
                c_ref[...] = jnp.zeros(c_ref.shape, F32)
                n_ref[...] = jnp.zeros(n_ref.shape, F32)
                m_ref[...] = jnp.zeros(m_ref.shape, F32)

        def rot(t):
            if not rope:
                return t
            parts = []
            for p in range(ML_DH // 128):
                tp = t[:, p * 128:(p + 1) * 128]
                parts.append(tp * cos_ref[:, p * 128:(p + 1) * 128]
                             + pltpu.roll(tp, 64, axis=1) * sin_ref[:, p * 128:(p + 1) * 128])
            return jnp.concatenate(parts, axis=-1)

        for h in range(ML_NH):
            hsl = slice(h * ML_DH, (h + 1) * ML_DH)
            q = rot(qkvo_ref[:, h * ML_DH:(h + 1) * ML_DH])
            k = rot(qkvo_ref[:, W_ML + h * ML_DH:W_ML + (h + 1) * ML_DH] * (ML_DH ** -0.5))
            v = qkvo_ref[:, 2 * W_ML + h * ML_DH:2 * W_ML + (h + 1) * ML_DH]
            qb, kb = q.astype(BF16), k.astype(BF16)
            m_prev = m_ref[0:1, h:h + 1]
            dmat = jnp.where(mask, b_c[:, h:h + 1] - b_r[h:h + 1, :] + i_r[h:h + 1, :], -jnp.inf)
            g0 = b_c[:, h:h + 1] + m_prev
            mt = jnp.maximum(g0, jnp.max(dmat, axis=1, keepdims=True))
            w = jnp.exp(dmat - mt)
            w0 = jnp.exp(g0 - mt)
            s = _dg(qb, kb, NT) * w
            ch = c_ref[h]
            nh = n_ref[h:h + 1, :]
            num = _dot(s.astype(BF16), v.astype(BF16)) + w0 * _dg(qb, ch.astype(BF16), NT)
            den = jnp.sum(s, axis=1, keepdims=True) + w0 * jnp.sum(q * nh, axis=1, keepdims=True)
            hb_ref[:, hsl] = num / jnp.maximum(jnp.abs(den), jnp.exp(-mt))
            d_end = b_end[:, h:h + 1] - b_c[:, h:h + 1] + i_c[:, h:h + 1]
            g0e = b_end[:, h:h + 1] + m_prev
            m_new = jnp.maximum(g0e, jnp.max(d_end, axis=0, keepdims=True))
            we = jnp.exp(d_end - m_new)
            w0e = jnp.exp(g0e - m_new)
            c_ref[h] = w0e * ch + _dg((v * we).astype(BF16), kb, TN)
            n_ref[h:h + 1, :] = w0e * nh + jnp.sum(we * k, axis=0, keepdims=True)
            m_ref[0:1, h:h + 1] = m_new

        @pl.when(is_last)
        def _():
            cl_ref[d] = c_ref[...]
            nl_ref[d] = n_ref[0:ML_NH, :]
            ml_ref[d:d + 1, :] = m_ref[0:1, 0:ML_NH]

        r0 = pl.multiple_of(ci * CHUNK, CHUNK)
        if d == 0:
            hf_ref[pl.ds(r0, CHUNK), :] = hb_ref[...]
        else:
            for h in range(ML_NH):
                hsl = slice(h * ML_DH, (h + 1) * ML_DH)
                t = hf_ref[pl.ds(r0, CHUNK), hsl] + hb_ref[:, hsl]
                mu = jnp.mean(t, axis=-1, keepdims=True)
                var = jnp.mean(jnp.square(t - mu), axis=-1, keepdims=True)
                og = jax.nn.sigmoid(qkvo_ref[:, 3 * W_ML + h * ML_DH:3 * W_ML + (h + 1) * ML_DH])
                o_ref[:, hsl] = (og * ((t - mu) * lax.rsqrt(var + 1e-5) * nw_ref[:, hsl])).astype(BF16)

    _two_phase(nc, run)


def _mlstm_call(proj, proj_q, gate_b, norm_w, rope_tabs, st0, base, n_seq, nc):
    init = st0 is not None
    rope = rope_tabs is not None
    L = nc * CHUNK
    if init:
        c0, n0, m0 = st0
    else:
        c0, n0, m0 = jnp.zeros((n_seq, 2, 1, 8, 128), F32), jnp.zeros((n_seq, 2, 8, 128), F32), jnp.zeros((n_seq, 2, ML_NH), F32)
    if rope:
        cos_t, sin_t = rope_tabs
        tab_spec = pl.BlockSpec((CHUNK, ML_DH), lambda b, j: (_chunk_of(j, nc), 0))
    else:
        cos_t = sin_t = jnp.zeros((8, 128), F32)
        tab_spec = pl.BlockSpec((8, 128), lambda b, j: (0, 0))
    gb_r, gb_c = gate_b.reshape(1, 4 * ML_NH), gate_b.reshape(4 * ML_NH, 1)

    def full(a):
        return pl.BlockSpec(a.shape, lambda b, j: (0,) * a.ndim)

    def per_seq(a):
        return pl.BlockSpec((None,) + a.shape[1:], lambda b, j: (b,) + (0,) * (a.ndim - 1))

    c_shape, n_shape = (ML_NH, ML_DH, ML_DH), (ML_NH, ML_DH)
    return pl.pallas_call(
        functools.partial(_mlstm_body, nc=nc, init=init, rope=rope),
        grid=(n_seq, 2 * nc),
        in_specs=[pl.BlockSpec((CHUNK, 4 * W_ML), _row_map(base, nc, 0)),
                  pl.BlockSpec((CHUNK, W_SMALL), _row_map(base, nc, O_SMALL // W_SMALL)),
                  full(gb_r), full(gb_c), full(norm_w), tab_spec, tab_spec,
                  per_seq(c0), per_seq(n0), per_seq(m0)],
        out_specs=[pl.BlockSpec((CHUNK, W_ML), _out_map(nc)),
                   pl.BlockSpec((None, 2) + c_shape, lambda b, j: (b, 0, 0, 0, 0)),
                   pl.BlockSpec((None, 2) + n_shape, lambda b, j: (b, 0, 0, 0)),
                   pl.BlockSpec((None, 2, ML_NH), lambda b, j: (b, 0, 0))],
        out_shape=[jax.ShapeDtypeStruct((n_seq * L, W_ML), BF16),
                   jax.ShapeDtypeStruct((n_seq, 2) + c_shape, F32),
                   jax.ShapeDtypeStruct((n_seq, 2) + n_shape, F32),
                   jax.ShapeDtypeStruct((n_seq, 2, ML_NH), F32)],
        scratch_shapes=[pltpu.VMEM((L, W_ML), F32), pltpu.VMEM((CHUNK, W_ML), F32), pltpu.VMEM(c_shape, F32),
                        pltpu.VMEM((8, ML_DH), F32), pltpu.VMEM((8, 128), F32)],
        compiler_params=_cparams(("parallel", "arbitrary")),
        name="mlstm",
    )(proj_q, proj, gb_r, gb_c, norm_w, cos_t, sin_t, c0, n0, m0)


def _hy_filt_body(ef_ref, eb_ref, tf_ref, tb_ref, w1_ref, b1_ref, w2_ref, b2_ref, w3f_ref, w3b_ref,
                  dcf_ref, dcb_ref, fa_ref, fb_ref, nyq_ref, hidf_ref, hidb_ref, *, L):
    @pl.when((pl.program_id(0) == 0) & (pl.program_id(1) == 0))
    def _():
        for e_ref, hid_ref in ((ef_ref, hidf_ref), (eb_ref, hidb_ref)):
            f = jnp.sin(_dot(e_ref[...], w1_ref[...]) + b1_ref[...])
            hid_ref[...] = jnp.sin(_dot(f.astype(BF16), w2_ref[...]) + b2_ref[...]).astype(BF16)

    def mlp(hid_ref, t_ref, w3_ref, dc_ref):
        return _dot(hid_ref[...], w3_ref[...]) * jnp.exp(-dc_ref[...] * t_ref[...])

    hf = mlp(hidf_ref, tf_ref, w3f_ref, dcf_ref)
    row = lax.broadcasted_iota(jnp.int32, hf.shape, 0)
    hb = jnp.where(row == 0, 0.0, mlp(hidb_ref, tb_ref, w3b_ref, dcb_ref))
    norm = jnp.sum(jnp.abs(hf), axis=0, keepdims=True) + jnp.sum(jnp.abs(hb), axis=0, keepdims=True) + 1e-6
    hf, hb = hf / norm, hb / norm
    sgn = (1 - 2 * (row & 1)).astype(F32)
    nyq_ref[...] = jnp.sum(sgn * (hf + hb), axis=0, keepdims=True)
    fa_ref[...] = hf.astype(BF16)
    fb_ref[...] = hb.astype(BF16)


def _hy_filt_call(L, emb_f, emb_b, t_f, t_b, w1, b1, w2, b2, w3, decay):
    tc = 256
    nct = W_HY // tc

    def full(a):
        return pl.BlockSpec(a.shape, lambda o, c: (0, 0))

    def col(dirn):
        return lambda o, c: (0, (2 * o + dirn) * nct + c)

    out_spec = pl.BlockSpec((L, tc), lambda o, c: (0, o * nct + c))
    return pl.pallas_call(
        functools.partial(_hy_filt_body, L=L),
        grid=(HY_ORDER, nct),
        in_specs=[full(emb_f), full(emb_b), full(t_f), full(t_b), full(w1), full(b1), full(w2), full(b2),
                  pl.BlockSpec((HY_FH, tc), col(0)), pl.BlockSpec((HY_FH, tc), col(1)),
                  pl.BlockSpec((1, tc), col(0)), pl.BlockSpec((1, tc), col(1))],
        out_specs=[out_spec, out_spec, pl.BlockSpec((1, tc), lambda o, c: (0, o * nct + c))],
        out_shape=[jax.ShapeDtypeStruct((L, HY_ORDER * W_HY), BF16), jax.ShapeDtypeStruct((L, HY_ORDER * W_HY), BF16),
                   jax.ShapeDtypeStruct((1, HY_ORDER * W_HY), F32)],
        scratch_shapes=[pltpu.VMEM((L, HY_FH), BF16), pltpu.VMEM((L, HY_FH), BF16)],
        compiler_params=_cparams(("arbitrary", "arbitrary")),
        name="hy_filt",
    )(emb_f, emb_b, t_f, t_b, w1, b1, w2, b2, w3, w3, decay, decay)


def _hy_spec_body(fa_ref, fb_ref, cz_ref, sz_ref, fre_ref, fim_ref, *, kb):
    cz, sz = cz_ref[...], sz_ref[...]
    a, b = fa_ref[...], fb_ref[...]
    kk = pl.program_id(1) * kb + lax.broadcasted_iota(jnp.int32, (kb, 1), 0)
    sgn = (1 - 2 * (kk & 1)).astype(F32)
    fre_ref[...] = (_dot(cz, a) + sgn * _dot(cz, b)).astype(BF16)
    fim_ref[...] = (-(_dot(sz, a) + sgn * _dot(sz, b))).astype(BF16)


def _hy_spec_call(L, kb, fa, fb, cz, sz):
    tc = 256
    n = fa.shape[1]
    return pl.pallas_call(
        functools.partial(_hy_spec_body, kb=kb),
        grid=(n // tc, L // kb),
        in_specs=[pl.BlockSpec((L, tc), lambda c, k: (0, c)), pl.BlockSpec((L, tc), lambda c, k: (0, c)),
                  pl.BlockSpec((kb, L), lambda c, k: (k, 0)), pl.BlockSpec((kb, L), lambda c, k: (k, 0))],
        out_specs=[pl.BlockSpec((kb, tc), lambda c, k: (k, c)), pl.BlockSpec((kb, tc), lambda c, k: (k, c))],
        out_shape=[jax.ShapeDtypeStruct((L, n), BF16), jax.ShapeDtypeStruct((L, n), BF16)],
        compiler_params=_cparams(("parallel", "parallel")),
        name="hy_spec",
    )(fa, fb, cz, sz)


def _hy_conv_body(v_ref, x1_ref, x2_ref, fre_ref, fim_ref, nyq_ref, bias_ref, cz_ref, sz_ref, czt_ref, szt_ref,
                  o_ref, z_ref, zb_ref, acc_ref, *, L, kb, nkb):
    o, k = pl.program_id(2), pl.program_id(3)
    inv_n = 1.0 / (2 * L)

    @pl.when((o == 0) & (k == 0))
    def _():
        z_ref[...] = v_ref[...]

    @pl.when(k == 0)
    def _():
        zb_ref[...] = z_ref[...].astype(BF16)
        acc_ref[...] = jnp.zeros_like(acc_ref)

    zb = zb_ref[...]
    zc = _dot(cz_ref[...], zb)
    zs = _dot(sz_ref[...], zb)
    fre, fim = fre_ref[...].astype(F32), fim_ref[...].astype(F32)
    kk = k * kb + lax.broadcasted_iota(jnp.int32, (kb, 1), 0)
    wk = jnp.where(kk == 0, inv_n, 2.0 * inv_n)
    yre = ((zc * fre + zs * fim) * wk).astype(BF16)
    yim = ((zc * fim - zs * fre) * wk).astype(BF16)
    acc_ref[...] += _dot(czt_ref[...], yre) - _dot(szt_ref[...], yim)

    @pl.when(k == nkb - 1)
    def _():
        z = z_ref[...]
        row = lax.broadcasted_iota(jnp.int32, (L, 1), 0)
        sgn = (1 - 2 * (row & 1)).astype(F32)
        z_nyq = jnp.sum(z * sgn, axis=0, keepdims=True)
        y = acc_ref[...] + sgn * (z_nyq * nyq_ref[...] * inv_n) + bias_ref[...] * z

        @pl.when(o == 0)
        def _():
            z_ref[...] = x1_ref[...] * y

        @pl.when(o == 1)
        def _():
            o_ref[...] = (x2_ref[...] * y).astype(BF16)


def _hy_conv_call(hu, fre, fim, nyq, bias, cz, sz, L, kb, tc, base, n_seq):
    nct = W_HY // tc
    nkb = L // kb

    def zin(part):
        return pl.BlockSpec((L, tc), lambda b, c, o, k: (base + b, part * nct + c))

    fspec = pl.BlockSpec((kb, tc), lambda b, c, o, k: (k, o * nct + c))
    vspec = pl.BlockSpec((1, tc), lambda b, c, o, k: (0, o * nct + c))
    return pl.pallas_call(
        functools.partial(_hy_conv_body, L=L, kb=kb, nkb=nkb),
        grid=(n_seq, nct, HY_ORDER, nkb),
        in_specs=[zin(0), zin(1), zin(2), fspec, fspec, vspec, vspec,
                  pl.BlockSpec((kb, L), lambda b, c, o, k: (k, 0)), pl.BlockSpec((kb, L), lambda b, c, o, k: (k, 0)),
                  pl.BlockSpec((L, kb), lambda b, c, o, k: (0, k)), pl.BlockSpec((L, kb), lambda b, c, o, k: (0, k))],
        out_specs=pl.BlockSpec((L, tc), lambda b, c, o, k: (b, c)),
        out_shape=jax.ShapeDtypeStruct((n_seq * L, W_HY), BF16),
        scratch_shapes=[pltpu.VMEM((L, tc), F32), pltpu.VMEM((L, tc), BF16), pltpu.VMEM((L, tc), F32)],
        compiler_params=_cparams(("parallel", "parallel", "arbitrary", "arbitrary")),
        name="hy_conv",
    )(hu, hu, hu, fre, fim, nyq, bias, cz, sz, cz, sz)


def _hyena_tables(L):
    t = np.arange(L, dtype=np.float32) / np.float32(L)

    def emb(tt):
        ang = (np.float32(2.0 * math.pi) * tt[:, None] * np.arange(1, HY_BANDS + 1, dtype=np.float32)[None, :]).astype(np.float32)
        e = np.concatenate([tt[:, None], np.cos(ang), np.sin(ang)], axis=-1).astype(np.float32)
        return np.pad(e, ((0, 0), (0, 128 - HY_EMB)))

    tb = np.concatenate([t[:1], t[:0:-1]])
    kk = np.arange(L, dtype=np.int64)
    ang = (2.0 * math.pi / (2 * L)) * ((kk[:, None] * kk[None, :]) % (2 * L)).astype(np.float64)
    return dict(emb_f=jnp.asarray(emb(t), BF16), emb_b=jnp.asarray(emb(tb), BF16),
                t_f=jnp.asarray(t[:, None]), t_b=jnp.asarray(tb[:, None]),
                cz=jnp.asarray(np.cos(ang), BF16), sz=jnp.asarray(np.sin(ang), BF16))


def _rope_tables(L):
    nf = ML_DH // 4
    freqs = (np.float32(ROPE_BASE) ** (-np.arange(nf, dtype=np.float32) / np.float32(nf))).astype(np.float32)
    pos = np.arange(L)
    out_c, out_s = [], []
    for p in (pos // GRID_W, pos % GRID_W):
        ang = p.astype(np.float32)[:, None] * freqs[None, :]
        c, s = np.cos(ang).astype(np.float32), np.sin(ang).astype(np.float32)
        out_c += [c, c]
        out_s += [-s, s]
    return jnp.asarray(np.concatenate(out_c, axis=-1)), jnp.asarray(np.concatenate(out_s, axis=-1))


def kernel(x_prompt, x_sample, state_rglru, state_ssd, state_mlstm_C, state_mlstm_n, state_mlstm_m, c, c_ctx,
           ada_w, ada_b, ln_g, ln_b, ffn_wg, ffn_wu, ffn_wd, w_in, rg_conv_w, rg_conv_b, rg_gate_w, rg_gate_b,
           rg_lambda, hy_conv_w, hy_conv_b, hy_w1, hy_b1, hy_w2, hy_b2, hy_w3, hy_decay, hy_bias,
           ssd_conv_w, ssd_conv_b, ssd_dt_bias, ssd_A_log, ssd_D, ssd_norm_w, ml_gate_b, ml_norm_w,
           branch_w, mix_out):
    bp, lp_len = x_prompt.shape[:2]
    bs, ls = x_sample.shape[:2]
    mp, ms = bp * lp_len, bs * ls
    ncp, ncs = lp_len // CHUNK, ls // CHUNK

    def midx(i, bm):
        return jnp.where(i < mp // bm, 0, 1 + (i - mp // bm) // (ls // bm))

    cond = jnp.concatenate([c_ctx[None], c, jnp.zeros((8 - 1 - bs, D_MODEL), F32)], axis=0)
    s = jax.nn.silu(cond).astype(BF16)
    mod = _ada_call(s, ada_w, ada_b.reshape(DEPTH, 1, N_MOD * D_MODEL))
    mod = mod[:, :1 + bs].reshape(DEPTH, 1 + bs, N_MOD, 1, D_MODEL)

    def mod_row(l, j):
        return mod[l, :, j]

    o = IN_OFFSETS

    w_t = jnp.swapaxes(w_in, 1, 2)

    def wseg(a, b):
        return w_t[:, a:b]

    w_small = jnp.concatenate(
        [wseg(o[1], o[2]), wseg(0, o[1]), wseg(o[2], o[4]),
         wseg(o[4], o[5]), wseg(o[6], o[7]),
         jnp.zeros((DEPTH, N_PROJ - O_SMALL - 2 * SSD_H - 4 * ML_NH, D_MODEL), F32)], axis=1).astype(BF16)
    w_qkvo = wseg(o[5], o[6]).astype(BF16)
    w_merge = wseg(o[7], w_in.shape[-1]).astype(BF16)
    wg00, wu00 = ffn_wg[0, 0].astype(BF16), ffn_wu[0, 0].astype(BF16)
    bw2d = branch_w.reshape(DEPTH, N_BRANCH * W_BRANCH, D_MODEL)
    gw_b = rg_gate_w.astype(BF16)
    hw1 = jnp.pad(hy_w1, ((0, 0), (0, 128 - HY_EMB), (0, 0))).astype(BF16)
    hw2, hw3 = hy_w2.astype(BF16), hy_w3.astype(BF16)

    tabs_p, tabs_s = _hyena_tables(lp_len), _hyena_tables(ls)
    rope_s = _rope_tables(ls)

    x = jnp.concatenate([x_prompt.reshape(mp, D_MODEL), x_sample.reshape(ms, D_MODEL)], axis=0)
    h = _modulate_call(x, mod_row(0, 0), mod_row(0, 1), midx)

    new_states = []
    wg1, wu1 = wg00, wu00
    for l in range(DEPTH):
        lng = ln_g[l].reshape(3, 1, D_MODEL)
        lnb = ln_b[l].reshape(3, 1, D_MODEL)

        a, (wd1,) = _ffn_up_call(h, wg1, wu1, (), sides=[(ffn_wd, (l, 0))])
        x, h = _res_ln_call(a, wd1, (), x, mod_row(l, 2), lng[0], lnb[0], mod_row(l, 3), mod_row(l, 4),
                            midx, 0.5, True)

        proj, (bw_l, mo_l) = _in_proj_call(h, w_small, (l,), sides=[(bw2d, (l,)), (mix_out, (l,))])
        proj_q, _ = _in_proj_call(h, w_qkvo, (l,))
        hu, xa, xbc = _conv_call(proj, hy_conv_w[l], hy_conv_b[l][None], rg_conv_w[l], rg_conv_b[l][None],
                                 ssd_conv_w[l], ssd_conv_b[l][None], mp // CHUNK, ncp, ncs)

        gb = rg_gate_b[l].reshape(4, W_RG)
        a_p, rg_last = _rglru_call(xa, proj, gw_b[l], gb, rg_lambda[l], None, 0, bp, ncp)
        a_s, _ = _rglru_call(xa, proj, gw_b[l], gb, rg_lambda[l], state_rglru[:, l], mp // CHUNK, bs, ncs)

        outs_b = []
        for tabs, L, kb, tc, base, n_seq in ((tabs_p, lp_len, lp_len, W_HY, 0, bp), (tabs_s, ls, 256, 256, mp // ls, bs)):
            fa, fb, nyq = _hy_filt_call(L, tabs['emb_f'], tabs['emb_b'], tabs['t_f'], tabs['t_b'], hw1[l], hy_b1[l][None],
                                        hw2[l], hy_b2[l][None], hw3[l], hy_decay[l][None])
            fre, fim = _hy_spec_call(L, kb, fa, fb, tabs['cz'], tabs['sz'])
            outs_b.append(_hy_conv_call(hu, fre, fim, nyq, hy_bias[l].reshape(1, HY_ORDER * W_HY), tabs['cz'], tabs['sz'],
                                        L, kb, tc, base, n_seq))

        dvec = jnp.repeat(ssd_D[l], SSD_P)[None]
        nw = ssd_norm_w[l][None]
        c_p, ssd_last = _ssd_call(xbc, proj, ssd_dt_bias[l], ssd_A_log[l], dvec, nw, None, 0, bp, ncp)
        c_s, _ = _ssd_call(xbc, proj, ssd_dt_bias[l], ssd_A_log[l], dvec, nw, state_ssd[:, l], mp // CHUNK, bs, ncs)

        mnw = ml_norm_w[l][None]
        d_p, cl, nl, ml = _mlstm_call(proj, proj_q, ml_gate_b[l], mnw, None, None, 0, bp, ncp)
        d_s, _, _, _ = _mlstm_call(proj, proj_q, ml_gate_b[l], mnw, rope_s,
                                   (state_mlstm_C[:, l], state_mlstm_n[:, l], state_mlstm_m[:, l]), mp // CHUNK, bs, ncs)
        new_states.append((rg_last, ssd_last, cl, nl, ml))

        merged, (wg2, wu2, wd2) = _merge_call(h, (a_p, outs_b[0], c_p, d_p), (a_s, outs_b[1], c_s, d_s), w_merge, (l,), bw_l,
                                              sides=[(ffn_wg, (l, 1)), (ffn_wu, (l, 1)), (ffn_wd, (l, 1))])
        x, h = _res_ln_call(merged, mo_l, (), x, mod_row(l, 5), lng[1], lnb[1], mod_row(l, 6), mod_row(l, 7),
                            midx, 1.0, True)

        last = l == DEPTH - 1
        a, cast = _ffn_up_call(h, wg2, wu2, (), sides=[] if last else [(ffn_wg, (l + 1, 0)), (ffn_wu, (l + 1, 0))])
        if not last:
            wg1, wu1 = cast
        nl_ = l if last else l + 1
        x, h = _res_ln_call(a, wd2, (), x, mod_row(l, 8), lng[2], lnb[2], mod_row(nl_, 0), mod_row(nl_, 1),
                            midx, 0.5, not last)

    yp = x[:mp].reshape(bp, lp_len, D_MODEL)
    ys = x[mp:].reshape(bs, ls, D_MODEL)
    outs = tuple(jnp.stack([new_states[l][j] for l in range(DEPTH)], axis=1) for j in range(5))
    return (yp, ys) + outs
```
